```python
import math
import jax, jax.numpy as jnp
from jax import lax
import numpy as np

D_MODEL = 1024
BATCH = 8
SEQ = 4096
DEPTH = 2

HEAD_DIM = 64
FOX_HEADS = 8
FOX_WIDTH = FOX_HEADS * HEAD_DIM
MLA_HEADS = 4
MLA_NOPE = 128
MLA_ROPE = 64
MLA_QK = MLA_NOPE + MLA_ROPE
MLA_V = 128
MLA_Q_RANK = 256
MLA_KV_RANK = 128
MLA_WIDTH = MLA_HEADS * MLA_V
MOBA_HEADS = 8
MOBA_WIDTH = MOBA_HEADS * HEAD_DIM
MOBA_BLOCK = 256
MOBA_TOPK = 3
MOBA_Q_CHUNK = 32
D_MIX = FOX_WIDTH + MLA_WIDTH + MOBA_WIDTH
Q_BLOCK = 128
ROPE_THETA = 10000.0
EPS = 1e-6
IN_SIZES = (FOX_WIDTH, FOX_WIDTH, FOX_WIDTH, FOX_HEADS, FOX_WIDTH,
            MLA_Q_RANK, MLA_KV_RANK, MLA_ROPE, MLA_WIDTH,
            MOBA_WIDTH, MOBA_WIDTH, MOBA_WIDTH, MOBA_WIDTH)
D_IN = sum(IN_SIZES)

kernel_name = "hybrid_fox_mla_moba_parallel_heads"


def rmsnorm(x, g):
    xf = x.astype(jnp.float32)
    y = xf * lax.rsqrt(jnp.mean(xf * xf, axis=-1, keepdims=True) + EPS)
    return (y * g.astype(jnp.float32)).astype(x.dtype)


def rope_tables(S):
    inv = ROPE_THETA ** (-jnp.arange(0, MLA_ROPE, 2, dtype=jnp.float32) / MLA_ROPE)
    ang = jnp.arange(S, dtype=jnp.float32)[:, None] * inv[None, :]
    ang = jnp.concatenate([ang, ang], axis=-1)
    return jnp.cos(ang), jnp.sin(ang)


def apply_rope(x, cos, sin):
    x1, x2 = jnp.split(x, 2, axis=-1)
    rot = jnp.concatenate([-x2, x1], axis=-1)
    return (x * cos + rot * sin).astype(x.dtype)


def causal_block_sweep(qh, kh, vh, decay=None):
    B, H, S, Dk = qh.shape
    scale = Dk ** -0.5
    kpos = jnp.arange(S)

    def one_block(i):
        start = i * Q_BLOCK
        qb = lax.dynamic_slice_in_dim(qh, start, Q_BLOCK, axis=2)
        s = jnp.einsum('bhqd,bhkd->bhqk', qb, kh, preferred_element_type=jnp.float32) * scale
        if decay is not None:
            db = lax.dynamic_slice_in_dim(decay, start, Q_BLOCK, axis=2)
            s = s + (db[..., :, None] - decay[..., None, :])
        qpos = start + jnp.arange(Q_BLOCK)
        s = jnp.where(qpos[:, None] >= kpos[None, :], s, -jnp.inf)
        p = jax.nn.softmax(s, axis=-1)
        return jnp.einsum('bhqk,bhkd->bhqd', p.astype(vh.dtype), vh)

    out = lax.map(one_block, jnp.arange(S // Q_BLOCK))
    Dv = out.shape[-1]
    return out.transpose(1, 0, 3, 2, 4).reshape(B, S, H * Dv)


def fox_attention(q, k, v, f_logit, b_f):
    B, S, _ = q.shape
    shp = (B, S, FOX_HEADS, HEAD_DIM)
    qh = q.reshape(shp).transpose(0, 2, 1, 3)
    kh = k.reshape(shp).transpose(0, 2, 1, 3)
    vh = v.reshape(shp).transpose(0, 2, 1, 3)
    log_f = jax.nn.log_sigmoid((f_logit + b_f).astype(jnp.float32))
    c = jnp.cumsum(log_f, axis=1).transpose(0, 2, 1)
    return causal_block_sweep(qh, kh, vh, decay=c)


def mla_attention(cq, ckv, kr, g_q, w_uq, g_kv, w_ukv, cos, sin):
    B, S, _ = cq.shape
    q = (rmsnorm(cq, g_q) @ w_uq).reshape(B, S, MLA_HEADS, MLA_QK)
    q_nope, q_rope = q[..., :MLA_NOPE], q[..., MLA_NOPE:]
    q = jnp.concatenate([q_nope, apply_rope(q_rope, cos[:, None, :], sin[:, None, :])], axis=-1)
    kv = (rmsnorm(ckv, g_kv) @ w_ukv).reshape(B, S, MLA_HEADS, MLA_NOPE + MLA_V)
    k_nope, v = kv[..., :MLA_NOPE], kv[..., MLA_NOPE:]
    k_rope = apply_rope(kr, cos, sin)
    k = jnp.concatenate([k_nope, jnp.broadcast_to(k_rope[:, :, None, :], (B, S, MLA_HEADS, MLA_ROPE)).astype(k_nope.dtype)], axis=-1)
    return causal_block_sweep(q.transpose(0, 2, 1, 3), k.transpose(0, 2, 1, 3), v.transpose(0, 2, 1, 3))


def moba_attention(q, k, v, slopes):
    B, S, _ = q.shape
    H, D, BLK = MOBA_HEADS, HEAD_DIM, MOBA_BLOCK
    shp = (B, S, H, D)
    qh = q.reshape(shp).transpose(0, 2, 1, 3)
    kh = k.reshape(shp).transpose(0, 2, 1, 3)
    vh = v.reshape(shp).transpose(0, 2, 1, 3)
    nb = -(-S // BLK)
    pad = nb * BLK - S
    kb = jnp.pad(kh, ((0, 0), (0, 0), (0, pad), (0, 0))).reshape(B, H, nb, BLK, D)
    vb = jnp.pad(vh, ((0, 0), (0, 0), (0, pad), (0, 0))).reshape(B, H, nb, BLK, D)
    kmean = jnp.mean(kb.astype(jnp.float32), axis=3)
    topk = min(MOBA_TOPK, nb)
    scale = D ** -0.5
    key_off = jnp.arange(BLK)
    bi = jnp.arange(B)[:, None, None, None]
    hi = jnp.arange(H)[None, :, None, None]
    m = slopes.astype(jnp.float32)

    def one_chunk(i):
        start = i * MOBA_Q_CHUNK
        qc = lax.dynamic_slice_in_dim(qh, start, MOBA_Q_CHUNK, axis=2)
        qpos = start + jnp.arange(MOBA_Q_CHUNK)
        cur = start // BLK
        gate = jnp.einsum('bhqd,bhnd->bhqn', qc.astype(jnp.float32), kmean)
        gate = jnp.where(jnp.arange(nb) < cur, gate, -jnp.inf)
        _, sel = lax.top_k(gate, topk)
        valid = sel < cur
        ks = kb[bi, hi, sel]
        vs = vb[bi, hi, sel]
        s_sel = jnp.einsum('bhqd,bhqnkd->bhqnk', qc, ks, preferred_element_type=jnp.float32) * scale
        kpos_sel = sel[..., None] * BLK + key_off
        dist_sel = (qpos[None, None, :, None, None] - kpos_sel).astype(jnp.float32)
        s_sel = s_sel - m[None, :, None, None, None] * dist_sel
        s_sel = jnp.where(valid[..., None], s_sel, -jnp.inf).reshape(B, H, MOBA_Q_CHUNK, topk * BLK)
        ko = lax.dynamic_index_in_dim(kb, cur, axis=2, keepdims=False)
        vo = lax.dynamic_index_in_dim(vb, cur, axis=2, keepdims=False)
        kpos_own = cur * BLK + key_off
        dist_own = (qpos[:, None] - kpos_own[None, :]).astype(jnp.float32)
        s_own = jnp.einsum('bhqd,bhkd->bhqk', qc, ko, preferred_element_type=jnp.float32) * scale
        s_own = s_own - m[None, :, None, None] * dist_own
        s_own = jnp.where(dist_own >= 0, s_own, -jnp.inf)
        p = jax.nn.softmax(jnp.concatenate([s_sel, s_own], axis=-1), axis=-1)
        p_sel = p[..., :topk * BLK].reshape(B, H, MOBA_Q_CHUNK, topk, BLK).astype(vs.dtype)
        p_own = p[..., topk * BLK:].astype(vo.dtype)
        return (jnp.einsum('bhqnk,bhqnkd->bhqd', p_sel, vs)
                + jnp.einsum('bhqk,bhkd->bhqd', p_own, vo))

    out = lax.map(one_chunk, jnp.arange(S // MOBA_Q_CHUNK))
    return out.transpose(1, 0, 3, 2, 4).reshape(B, S, H * D)


def hybrid_layer(x, ln_g, w_in, b_f, g_q, w_uq, g_kv, w_ukv, out_g, w_out, cos, sin, slopes):
    h = rmsnorm(x, ln_g)
    proj = h @ w_in
    cuts, acc = [], 0
    for sz in IN_SIZES[:-1]:
        acc += sz
        cuts.append(acc)
    (fq, fk, fv, ff, fz, cq, ckv, kr, mz, bq, bk, bv, bz) = jnp.split(proj, cuts, axis=-1)
    y_a = fox_attention(fq, fk, fv, ff, b_f)
    y_b = mla_attention(cq, ckv, kr, g_q, w_uq, g_kv, w_ukv, cos, sin)
    y_c = moba_attention(bq, bk, bv, slopes)
    g_a, g_b, g_c = jnp.split(out_g, [FOX_WIDTH, FOX_WIDTH + MLA_WIDTH])
    y = jnp.concatenate([rmsnorm(y_a, g_a) * jax.nn.silu(fz),
                         rmsnorm(y_b, g_b) * jax.nn.silu(mz),
                         rmsnorm(y_c, g_c) * jax.nn.silu(bz)], axis=-1)
    return x + y @ w_out


def setup_inputs(seed: int = 0) -> dict:
    key = jax.random.key(seed)
    ks = jax.random.split(key, 12)
    f32 = jnp.float32
    nrm = lambda k, shp: jax.random.normal(k, shp, dtype=f32)
    return {
        "x": nrm(ks[0], (BATCH, SEQ, D_MODEL)),
        "ln_g": 1.0 + 0.02 * nrm(ks[1], (DEPTH, D_MODEL)),
        "w_in": nrm(ks[2], (DEPTH, D_MODEL, D_IN)) * D_MODEL ** -0.5,
        "fox_b_f": 2.0 + 0.5 * nrm(ks[3], (DEPTH, FOX_HEADS)),
        "mla_q_g": 1.0 + 0.02 * nrm(ks[4], (DEPTH, MLA_Q_RANK)),
        "mla_w_uq": nrm(ks[5], (DEPTH, MLA_Q_RANK, MLA_HEADS * MLA_QK)) * MLA_Q_RANK ** -0.5,
        "mla_kv_g": 1.0 + 0.02 * nrm(ks[6], (DEPTH, MLA_KV_RANK)),
        "mla_w_ukv": nrm(ks[7], (DEPTH, MLA_KV_RANK, MLA_HEADS * (MLA_NOPE + MLA_V))) * MLA_KV_RANK ** -0.5,
        "out_g": 1.0 + 0.02 * nrm(ks[8], (DEPTH, D_MIX)),
        "w_out": nrm(ks[9], (DEPTH, D_MIX, D_MODEL)) * D_MIX ** -0.5,
        "final_g": 1.0 + 0.02 * nrm(ks[10], (D_MODEL,)),
    }


def reference(x, ln_g, w_in, fox_b_f, mla_q_g, mla_w_uq, mla_kv_g, mla_w_ukv, out_g, w_out, final_g):
    S = x.shape[1]
    cos, sin = rope_tables(S)
    slopes = 2.0 ** (-8.0 * jnp.arange(1, MOBA_HEADS + 1, dtype=jnp.float32) / MOBA_HEADS)
    for l in range(DEPTH):
        x = hybrid_layer(x, ln_g[l], w_in[l], fox_b_f[l], mla_q_g[l], mla_w_uq[l],
                         mla_kv_g[l], mla_w_ukv[l], out_g[l], w_out[l], cos, sin, slopes)
    return rmsnorm(x, final_g)
```

```python
import functools
import math

import jax
import jax.numpy as jnp
import numpy as np
from jax import lax
from jax.experimental import pallas as pl
from jax.experimental.pallas import tpu as pltpu

D_MODEL = 1024
HEAD_DIM = 64
FOX_HEADS = 8
FOX_WIDTH = FOX_HEADS * HEAD_DIM
MLA_HEADS = 4
MLA_NOPE = 128
MLA_ROPE = 64
MLA_QK = MLA_NOPE + MLA_ROPE
MLA_V = 128
MLA_Q_RANK = 256
MLA_KV_RANK = 128
MLA_WIDTH = MLA_HEADS * MLA_V
MOBA_HEADS = 8
MOBA_WIDTH = MOBA_HEADS * HEAD_DIM
MOBA_BLOCK = 256
MOBA_TOPK = 3
D_MIX = FOX_WIDTH + MLA_WIDTH + MOBA_WIDTH
ROPE_THETA = 10000.0
EPS = 1e-6
IN_SIZES = (FOX_WIDTH, FOX_WIDTH, FOX_WIDTH, FOX_HEADS, FOX_WIDTH,
            MLA_Q_RANK, MLA_KV_RANK, MLA_ROPE, MLA_WIDTH,
            MOBA_WIDTH, MOBA_WIDTH, MOBA_WIDTH, MOBA_WIDTH)

LANES = 128
BF16_SUBLANES = 16
VMEM_LIMIT_BYTES = 48 * 1024 * 1024

COL_FQ, COL_FK, COL_FV, COL_FZ = 0, 4, 8, 12
COL_BQ, COL_BK, COL_BV, COL_BZ = 16, 20, 24, 28
COL_MZ, COL_CQ, COL_CKV, COL_KR = 32, 36, 38, 39
PACKED_COLS = 40 * LANES

MLA_QCAT = 2 * LANES
NEG = -1e30

F32 = jnp.float32
BF16 = jnp.bfloat16


def _rms(x, g):
    return x * lax.rsqrt(jnp.mean(x * x, axis=-1, keepdims=True) + EPS) * g


def _nt_dot(a, b):
    return lax.dot_general(a, b, (((1,), (1,)), ((), ())), preferred_element_type=F32)


def _inproj_kernel(x_ref, g_ref, w_ref, wff_ref, bf_ref, proj_ref, c_ref, h_scr, carry_scr,
                   *, tiles_per_seq):
    i = pl.program_id(0)
    j = pl.program_id(1)

    @pl.when(j == 0)
    def _():
        h = _rms(x_ref[...], g_ref[...]).astype(BF16)
        h_scr[...] = h
        logit = _nt_dot(wff_ref[...], h) + bf_ref[...]
        log_f = jnp.minimum(logit, 0.0) - jnp.log1p(jnp.exp(-jnp.abs(logit)))
        tm = log_f.shape[1]
        lane = lax.broadcasted_iota(jnp.int32, log_f.shape, 1)
        c = log_f
        d = 1
        while d < tm:
            c = c + jnp.where(lane >= d, pltpu.roll(c, d, axis=1), 0.0)
            d *= 2

        @pl.when(i % tiles_per_seq == 0)
        def _():
            carry_scr[...] = jnp.zeros_like(carry_scr)

        c = c + carry_scr[...][:, :1]
        c_ref[0] = c[:FOX_HEADS]
        carry_scr[...] = jnp.broadcast_to(c[:, tm - 1:tm], carry_scr.shape)

    proj_ref[...] = jnp.dot(h_scr[...], w_ref[...],
                            preferred_element_type=F32).astype(proj_ref.dtype)


def _inproj(x2d, ln_g, w_packed, wff_t, bf_col, *, batch, seq):
    tokens = x2d.shape[0]
    tm, tn = 1024, 1024
    tiles_per_seq = seq // tm
    grid = (tokens // tm, PACKED_COLS // tn)
    return pl.pallas_call(
        functools.partial(_inproj_kernel, tiles_per_seq=tiles_per_seq),
        grid=grid,
        in_specs=[
            pl.BlockSpec((tm, D_MODEL), lambda i, j: (i, 0)),
            pl.BlockSpec((1, D_MODEL), lambda i, j: (0, 0)),
            pl.BlockSpec((D_MODEL, tn), lambda i, j: (0, j)),
            pl.BlockSpec((BF16_SUBLANES, D_MODEL), lambda i, j: (0, 0)),
            pl.BlockSpec((BF16_SUBLANES, 1), lambda i, j: (0, 0)),
        ],
        out_specs=[
            pl.BlockSpec((tm, tn), lambda i, j: (i, j)),
            pl.BlockSpec((1, FOX_HEADS, tm),
                         lambda i, j: (i // tiles_per_seq, 0, i % tiles_per_seq)),
        ],
        out_shape=[
            jax.ShapeDtypeStruct((tokens, PACKED_COLS), BF16),
            jax.ShapeDtypeStruct((batch, FOX_HEADS, seq), F32),
        ],
        scratch_shapes=[
            pltpu.VMEM((tm, D_MODEL), BF16),
            pltpu.VMEM((BF16_SUBLANES, LANES), F32),
        ],
        compiler_params=pltpu.CompilerParams(
            dimension_semantics=("arbitrary", "arbitrary"),
            vmem_limit_bytes=VMEM_LIMIT_BYTES),
        name="inproj",
    )(x2d, ln_g, w_packed, wff_t, bf_col)


def _mla_up_kernel(cq_ref, ckv_ref, krp_ref, cs_ref, gq_ref, gkv_ref, wq_ref, wkv_ref,
                   q_ref, k_ref, v_ref):
    lane = lax.broadcasted_iota(jnp.int32, (1, LANES), 1)
    cs = cs_ref[...]
    scale = MLA_QK ** -0.5

    def rope_pair(a):
        t = a * cs
        return t + pltpu.roll(t, LANES // 2, axis=1)

    nq = _rms(cq_ref[...].astype(F32), gq_ref[...]).astype(BF16)
    qa = jnp.dot(nq, wq_ref[...], preferred_element_type=F32)
    for h in range(MLA_HEADS):
        base = h * MLA_QCAT
        q_ref[:, base:base + LANES] = (qa[:, base:base + LANES] * scale).astype(q_ref.dtype)
        rp = rope_pair(qa[:, base + LANES:base + MLA_QCAT])
        rp = jnp.where(lane < MLA_ROPE, rp, 0.0)
        q_ref[:, base + LANES:base + MLA_QCAT] = (rp * scale).astype(q_ref.dtype)

    nkv = _rms(ckv_ref[...].astype(F32), gkv_ref[...]).astype(BF16)
    kv = jnp.dot(nkv, wkv_ref[...], preferred_element_type=F32)
    kr = rope_pair(krp_ref[...].astype(F32)).astype(k_ref.dtype)
    for h in range(MLA_HEADS):
        base = h * MLA_QCAT
        k_ref[:, base:base + LANES] = kv[:, h * LANES:(h + 1) * LANES].astype(k_ref.dtype)
        k_ref[:, base + LANES:base + MLA_QCAT] = kr
    v_ref[...] = kv[:, MLA_HEADS * LANES:].astype(v_ref.dtype)


def _mla_up(proj, cs, g_q, g_kv, wq_packed, wkv_packed, *, seq):
    tokens = proj.shape[0]
    tm = 1024
    tiles_per_seq = seq // tm
    qk_cols = MLA_HEADS * MLA_QCAT
    return pl.pallas_call(
        _mla_up_kernel,
        grid=(tokens // tm,),
        in_specs=[
            pl.BlockSpec((tm, MLA_Q_RANK), lambda i: (i, COL_CQ * LANES // MLA_Q_RANK)),
            pl.BlockSpec((tm, LANES), lambda i: (i, COL_CKV)),
            pl.BlockSpec((tm, LANES), lambda i: (i, COL_KR)),
            pl.BlockSpec((tm, LANES), lambda i: (i % tiles_per_seq, 0)),
            pl.BlockSpec((1, MLA_Q_RANK), lambda i: (0, 0)),
            pl.BlockSpec((1, MLA_KV_RANK), lambda i: (0, 0)),
            pl.BlockSpec((MLA_Q_RANK, qk_cols), lambda i: (0, 0)),
            pl.BlockSpec((MLA_KV_RANK, 2 * MLA_WIDTH), lambda i: (0, 0)),
        ],
        out_specs=[
            pl.BlockSpec((tm, qk_cols), lambda i: (i, 0)),
            pl.BlockSpec((tm, qk_cols), lambda i: (i, 0)),
            pl.BlockSpec((tm, MLA_WIDTH), lambda i: (i, 0)),
        ],
        out_shape=[
            jax.ShapeDtypeStruct((tokens, qk_cols), BF16),
            jax.ShapeDtypeStruct((tokens, qk_cols), BF16),
            jax.ShapeDtypeStruct((tokens, MLA_WIDTH), BF16),
        ],
        compiler_params=pltpu.CompilerParams(
            dimension_semantics=("arbitrary",),
            vmem_limit_bytes=VMEM_LIMIT_BYTES),
        name="mla_up",
    )(proj, proj, proj, cs, g_q, g_kv, wq_packed, wkv_packed)


def _softmax_step(q, k, v, state, row_bias=None, col_bias=None, mask=None):
    m, l, acc = state
    s = _nt_dot(q, k)
    if row_bias is not None:
        s = s + row_bias
    if col_bias is not None:
        s = s + col_bias
    if mask is not None:
        s = jnp.where(mask, s, NEG)
    m_new = jnp.maximum(m, jnp.max(s, axis=1, keepdims=True))
    alpha = jnp.exp(m - m_new)
    p = jnp.exp(s - m_new)
    l = alpha * l + jnp.sum(p, axis=1, keepdims=True)
    acc = alpha * acc + jnp.dot(p.astype(v.dtype), v, preferred_element_type=F32)
    return m_new, l, acc


def _init_state(tq):
    return (jnp.full((tq, 1), NEG, F32), jnp.zeros((tq, 1), F32), jnp.zeros((tq, LANES), F32))


def _split_head_pair(q):
    lane = lax.broadcasted_iota(jnp.int32, (1, LANES), 1)
    zero = jnp.zeros_like(q)
    return [jnp.where(lane < HEAD_DIM, q, zero), jnp.where(lane >= HEAD_DIM, q, zero)]


def _merge_head_pair(states):
    lane = lax.broadcasted_iota(jnp.int32, (1, LANES), 1)
    (_, l0, a0), (_, l1, a1) = states
    return jnp.where(lane < HEAD_DIM, a0 / l0, a1 / l1)


def _causal_mask(t):
    row = lax.broadcasted_iota(jnp.int32, (t, t), 0)
    col = lax.broadcasted_iota(jnp.int32, (t, t), 1)
    return col <= row


def _fox_kernel(q_ref, k_ref, v_ref, c_ref, o_ref, *, tq):
    g = pl.program_id(1)
    qi = pl.program_id(2)
    qs = _split_head_pair(q_ref[0] * jnp.asarray(HEAD_DIM ** -0.5, q_ref.dtype))

    def bias(sub, start):
        return -c_ref[0, pl.ds(2 * g + sub, 1), pl.ds(start, tq)]

    def step(start, states, mask):
        k = k_ref[0, pl.ds(start, tq), :]
        v = v_ref[0, pl.ds(start, tq), :]
        return tuple(_softmax_step(qs[sub], k, v, states[sub], row_bias=bias(sub, start), mask=mask)
                     for sub in range(2))

    states = step(pl.multiple_of(qi * tq, tq), (_init_state(tq), _init_state(tq)), _causal_mask(tq))
    states = lax.fori_loop(
        0, qi, lambda kt, st: step(pl.multiple_of(kt * tq, tq), st, None), states)
    o_ref[0] = _merge_head_pair(states).astype(o_ref.dtype)


def _mla_kernel(q_ref, k_ref, v_ref, o_ref, *, tq):
    qi = pl.program_id(2)
    q = q_ref[0]

    def step(start, state, mask):
        k = k_ref[0, pl.ds(start, tq), :]
        v = v_ref[0, pl.ds(start, tq), :]
        return _softmax_step(q, k, v, state, mask=mask)

    state = step(pl.multiple_of(qi * tq, tq), _init_state(tq), _causal_mask(tq))
    state = lax.fori_loop(
        0, qi, lambda kt, st: step(pl.multiple_of(kt * tq, tq), st, None), state)
    _, l, acc = state
    o_ref[0] = (acc / l).astype(o_ref.dtype)


def _moba_kernel(q_ref, k_ref, v_ref, slope_ref, o_ref, kmean_scr, sel_scr, *, n_blocks):
    blk = MOBA_BLOCK
    g = pl.program_id(1)
    qi = pl.program_id(2)

    @pl.when(qi == 0)
    def _():
        for n in range(n_blocks):
            kb = k_ref[0, n * blk:(n + 1) * blk, :].astype(F32)
            mean = jnp.sum(kb, axis=0, keepdims=True) * (1.0 / blk)
            hi = mean.astype(BF16)
            mid = (mean - hi.astype(F32)).astype(BF16)
            lo = (mean - hi.astype(F32) - mid.astype(F32)).astype(BF16)
            kmean_scr[0, n:n + 1, :] = hi.astype(F32)
            kmean_scr[1, n:n + 1, :] = mid.astype(F32)
            kmean_scr[2, n:n + 1, :] = lo.astype(F32)

    q_raw = _split_head_pair(q_ref[0])
    qs = [q * jnp.asarray(HEAD_DIM ** -0.5, q.dtype) for q in q_raw]

    col = lax.broadcasted_iota(jnp.int32, (blk, n_blocks), 1)
    for sub in range(2):
        gate = (_nt_dot(q_raw[sub], kmean_scr[0].astype(BF16))
                + _nt_dot(q_raw[sub], kmean_scr[1].astype(BF16))
                + _nt_dot(q_raw[sub], kmean_scr[2].astype(BF16)))
        rank = jnp.zeros((blk, n_blocks), jnp.int32)
        for m in range(n_blocks - 1):
            gm = gate[:, m:m + 1]
            ahead = (gm > gate) | ((gm == gate) & (m < col))
            rank = rank + jnp.where(ahead, jnp.where(m < qi, 1, 0), 0)
        sel_bias = jnp.where((rank < MOBA_TOPK) & (col < qi), 0.0, NEG)
        for n in range(n_blocks - 1):
            sel_scr[sub, n] = jnp.broadcast_to(sel_bias[:, n:n + 1], (blk, LANES))

    q_start = qi * blk
    key_off = lax.broadcasted_iota(jnp.int32, (1, blk), 1)

    def bias(sub, start):
        slope = slope_ref[pl.ds(2 * g + sub, 1), :]
        slope = jnp.concatenate([slope] * (blk // LANES), axis=1)
        return slope * (start + key_off - q_start).astype(F32)

    def step(start, states, mask, sel):
        k = k_ref[0, pl.ds(start, blk), :]
        v = v_ref[0, pl.ds(start, blk), :]
        return tuple(_softmax_step(qs[sub], k, v, states[sub], row_bias=bias(sub, start),
                                   col_bias=None if sel is None else sel[sub], mask=mask)
                     for sub in range(2))

    states = step(pl.multiple_of(q_start, blk), (_init_state(blk), _init_state(blk)),
                  _causal_mask(blk), None)

    def past(n, st):
        sel = [jnp.concatenate([sel_scr[sub, n]] * (blk // LANES), axis=1) for sub in range(2)]
        return step(pl.multiple_of(n * blk, blk), st, None, sel)

    states = lax.fori_loop(0, qi, past, states)
    o_ref[0] = _merge_head_pair(states).astype(o_ref.dtype)


def _attention(kind, q_arr, k_arr, v_arr, extra, *, batch, seq, q_col, k_col, v_col):
    if kind == "mla":
        groups, dk, tq = MLA_HEADS, MLA_QCAT, 512
    else:
        groups, dk, tq = FOX_HEADS // 2, LANES, (512 if kind == "fox" else MOBA_BLOCK)
    in_specs = [
        pl.BlockSpec((1, tq, dk), lambda b, g, qi: (b, qi, q_col + g)),
        pl.BlockSpec((1, seq, dk), lambda b, g, qi: (b, 0, k_col + g)),
        pl.BlockSpec((1, seq, LANES), lambda b, g, qi: (b, 0, v_col + g)),
    ]
    operands = [q_arr, k_arr, v_arr]
    scratch = []
    if kind == "fox":
        body = functools.partial(_fox_kernel, tq=tq)
        in_specs.append(pl.BlockSpec((1, FOX_HEADS, seq), lambda b, g, qi: (b, 0, 0)))
        operands.append(extra)
    elif kind == "mla":
        body = functools.partial(_mla_kernel, tq=tq)
    else:
        n_blocks = seq // MOBA_BLOCK
        body = functools.partial(_moba_kernel, n_blocks=n_blocks)
        in_specs.append(pl.BlockSpec((MOBA_HEADS, LANES), lambda b, g, qi: (0, 0)))
        operands.append(extra)
        scratch = [pltpu.VMEM((3, n_blocks, LANES), F32),
                   pltpu.VMEM((2, n_blocks - 1, MOBA_BLOCK, LANES), F32)]
    return pl.pallas_call(
        body,
        grid=(batch, groups, seq // tq),
        in_specs=in_specs,
        out_specs=pl.BlockSpec((1, tq, LANES), lambda b, g, qi: (b, qi, g)),
        out_shape=jax.ShapeDtypeStruct((batch, seq, groups * LANES), BF16),
        scratch_shapes=scratch,
        compiler_params=pltpu.CompilerParams(
            dimension_semantics=("arbitrary", "arbitrary", "arbitrary"),
            vmem_limit_bytes=VMEM_LIMIT_BYTES),
        name=kind + "_attn",
    )(*operands)


def _combine_kernel(ya_ref, yb_ref, yc_ref, za_ref, zb_ref, zc_ref, x_ref, og_ref, w_ref, fg_ref,
                    o_ref, *, final):
    acc = x_ref[...]
    groups = ((ya_ref, za_ref), (yb_ref, zb_ref), (yc_ref, zc_ref))
    for idx, (y_ref, z_ref) in enumerate(groups):
        lo, hi = idx * FOX_WIDTH, (idx + 1) * FOX_WIDTH
        z = z_ref[...].astype(F32)
        gated = _rms(y_ref[...].astype(F32), og_ref[:, lo:hi]) * (z / (1.0 + jnp.exp(-z)))
        acc = acc + jnp.dot(gated.astype(BF16), w_ref[lo:hi, :], preferred_element_type=F32)
    if final:
        acc = _rms(acc, fg_ref[...])
    o_ref[...] = acc


def _combine(ya, yb, yc, proj, x2d, out_g, w_out, final_g, *, final):
    tokens = x2d.shape[0]
    tm = 512
    width = FOX_WIDTH
    zcol = lambda c: c * LANES // width
    y_spec = pl.BlockSpec((tm, width), lambda i: (i, 0))
    return pl.pallas_call(
        functools.partial(_combine_kernel, final=final),
        grid=(tokens // tm,),
        in_specs=[
            y_spec, y_spec, y_spec,
            pl.BlockSpec((tm, width), lambda i: (i, zcol(COL_FZ))),
            pl.BlockSpec((tm, width), lambda i: (i, zcol(COL_MZ))),
            pl.BlockSpec((tm, width), lambda i: (i, zcol(COL_BZ))),
            pl.BlockSpec((tm, D_MODEL), lambda i: (i, 0)),
            pl.BlockSpec((1, D_MIX), lambda i: (0, 0)),
            pl.BlockSpec((D_MIX, D_MODEL), lambda i: (0, 0)),
            pl.BlockSpec((1, D_MODEL), lambda i: (0, 0)),
        ],
        out_specs=pl.BlockSpec((tm, D_MODEL), lambda i: (i, 0)),
        out_shape=jax.ShapeDtypeStruct((tokens, D_MODEL), F32),
        compiler_params=pltpu.CompilerParams(
            dimension_semantics=("arbitrary",),
            vmem_limit_bytes=VMEM_LIMIT_BYTES),
        name="combine",
    )(ya, yb, yc, proj, proj, proj, x2d, out_g, w_out, final_g)


def _rot_cols(w):
    half = MLA_ROPE // 2
    return jnp.concatenate([-w[..., half:], w[..., :half]], axis=-1)


def _pack_layer(w_in, w_uq, w_ukv):
    cuts = np.cumsum(IN_SIZES)[:-1].tolist()
    fq, fk, fv, ff, fz, cq, ckv, kr, mz, bq, bk, bv, bz = jnp.split(w_in, cuts, axis=1)
    w_packed = jnp.concatenate(
        [fq, fk, fv, fz, bq, bk, bv, bz, mz, cq, ckv, kr, _rot_cols(kr)], axis=1).astype(BF16)
    wff_t = jnp.zeros((BF16_SUBLANES, D_MODEL), F32).at[:FOX_HEADS].set(ff.T).astype(BF16)
    wq = w_uq.reshape(MLA_Q_RANK, MLA_HEADS, MLA_QK)
    nope, rope = wq[..., :MLA_NOPE], wq[..., MLA_NOPE:]
    wq_packed = jnp.concatenate([nope, rope, _rot_cols(rope)], axis=-1)
    wq_packed = wq_packed.reshape(MLA_Q_RANK, MLA_HEADS * MLA_QCAT).astype(BF16)
    wkv = w_ukv.reshape(MLA_KV_RANK, MLA_HEADS, MLA_NOPE + MLA_V)
    wkv_packed = jnp.concatenate(
        [wkv[..., :MLA_NOPE].reshape(MLA_KV_RANK, MLA_WIDTH),
         wkv[..., MLA_NOPE:].reshape(MLA_KV_RANK, MLA_WIDTH)], axis=1).astype(BF16)
    return w_packed, wff_t, wq_packed, wkv_packed


def kernel(x, ln_g, w_in, fox_b_f, mla_q_g, mla_w_uq, mla_kv_g, mla_w_ukv, out_g, w_out, final_g):
    batch, seq, _ = x.shape
    depth = ln_g.shape[0]
    tokens = batch * seq

    inv = ROPE_THETA ** (-jnp.arange(0, MLA_ROPE, 2, dtype=F32) / MLA_ROPE)
    ang = jnp.arange(seq, dtype=F32)[:, None] * inv[None, :]
    ang = jnp.concatenate([ang, ang], axis=-1)
    cs = jnp.concatenate([jnp.cos(ang), jnp.sin(ang)], axis=-1)
    slopes = 2.0 ** (-8.0 * jnp.arange(1, MOBA_HEADS + 1, dtype=F32) / MOBA_HEADS)
    slopes = jnp.broadcast_to(slopes[:, None], (MOBA_HEADS, LANES))

    x2d = x.reshape(tokens, D_MODEL)
    for l in range(depth):
        w_packed, wff_t, wq_packed, wkv_packed = _pack_layer(w_in[l], mla_w_uq[l], mla_w_ukv[l])
        bf_col = jnp.zeros((BF16_SUBLANES, 1), F32).at[:FOX_HEADS, 0].set(fox_b_f[l])
        proj, c = _inproj(x2d, ln_g[l][None, :], w_packed, wff_t, bf_col, batch=batch, seq=seq)
        q_cat, k_cat, v_mla = _mla_up(proj, cs, mla_q_g[l][None, :], mla_kv_g[l][None, :],
                                      wq_packed, wkv_packed, seq=seq)
        proj3 = proj.reshape(batch, seq, PACKED_COLS)
        ya = _attention("fox", proj3, proj3, proj3, c, batch=batch, seq=seq,
                        q_col=COL_FQ, k_col=COL_FK, v_col=COL_FV)
        yb = _attention("mla", q_cat.reshape(batch, seq, -1), k_cat.reshape(batch, seq, -1),
                        v_mla.reshape(batch, seq, -1), None, batch=batch, seq=seq,
                        q_col=0, k_col=0, v_col=0)
        yc = _attention("moba", proj3, proj3, proj3, slopes, batch=batch, seq=seq,
                        q_col=COL_BQ, k_col=COL_BK, v_col=COL_BV)
        x2d = _combine(ya.reshape(tokens, -1), yb.reshape(tokens, -1), yc.reshape(tokens, -1),
                       proj, x2d, out_g[l][None, :], w_out[l].astype(BF16), final_g[None, :],
                       final=(l == depth - 1))
    return x2d.reshape(batch, seq, D_MODEL)
```

```python
import functools

import jax
import jax.numpy as jnp
import numpy as np
from jax import lax
from jax.experimental import pallas as pl
from jax.experimental.pallas import tpu as pltpu

D_MODEL = 1024
HEAD_DIM = 64
FOX_HEADS = 8
FOX_WIDTH = FOX_HEADS * HEAD_DIM
MLA_HEADS = 4
MLA_NOPE = 128
MLA_ROPE = 64
MLA_QK = MLA_NOPE + MLA_ROPE
MLA_V = 128
MLA_Q_RANK = 256
MLA_KV_RANK = 128
MLA_WIDTH = MLA_HEADS * MLA_V
MOBA_HEADS = 8
MOBA_WIDTH = MOBA_HEADS * HEAD_DIM
MOBA_BLOCK = 256
MOBA_TOPK = 3
D_MIX = FOX_WIDTH + MLA_WIDTH + MOBA_WIDTH
ROPE_THETA = 10000.0
EPS = 1e-6
IN_SIZES = (FOX_WIDTH, FOX_WIDTH, FOX_WIDTH, FOX_HEADS, FOX_WIDTH,
            MLA_Q_RANK, MLA_KV_RANK, MLA_ROPE, MLA_WIDTH,
            MOBA_WIDTH, MOBA_WIDTH, MOBA_WIDTH, MOBA_WIDTH)

LANES = 128
SUBLANES = 8
BF16_SUBLANES = 16
MXU_DEPTH = 256
VMEM_LIMIT_BYTES = 48 * 1024 * 1024

COL_FQ, COL_FK, COL_FV, COL_FZ = 0, 4, 8, 12
COL_BQ, COL_BK, COL_BV, COL_BZ = 16, 20, 24, 28
COL_MZ, COL_CQ, COL_CKV, COL_KR = 32, 36, 38, 39
PACKED_COLS = 40 * LANES

QK_DEPTH = MXU_DEPTH
N_TERMS = 3
NEG = -1e30
LOG2E = 1.4426950408889634
ATTN_TILE = 512

F32 = jnp.float32
BF16 = jnp.bfloat16


def _rms(x, g):
    return x * lax.rsqrt(jnp.mean(x * x, axis=-1, keepdims=True) + EPS) * g


def _nt_dot(a, b):
    return lax.dot_general(a, b, (((1,), (1,)), ((), ())), preferred_element_type=F32)


def _bf16_terms(x):
    hi = x.astype(BF16)
    mid = (x - hi.astype(F32)).astype(BF16)
    lo = (x - hi.astype(F32) - mid.astype(F32)).astype(BF16)
    return hi, mid, lo


def _transpose_to_rows(cols_t):
    n = cols_t.shape[1]
    blk = MXU_DEPTH
    eye = (lax.broadcasted_iota(jnp.int32, (blk, blk), 0)
           == lax.broadcasted_iota(jnp.int32, (blk, blk), 1)).astype(BF16)
    return jnp.concatenate(
        [_nt_dot(eye, cols_t[:, s:s + blk]) for s in range(0, n, blk)], axis=0)


def _inproj_kernel(x_ref, g_ref, w_ref, wff_ref, bf_ref, proj_ref, caux_ref, h_scr, carry_scr,
                   *, tiles_per_seq):
    i = pl.program_id(0)
    j = pl.program_id(1)

    @pl.when(j == 0)
    def _():
        h = _rms(x_ref[...], g_ref[...]).astype(BF16)
        h_scr[...] = h
        logit = _nt_dot(wff_ref[...], h) + bf_ref[...]
        log_f = jnp.minimum(logit, 0.0) - jnp.log1p(jnp.exp(-jnp.abs(logit)))
        tm = log_f.shape[1]
        lane = lax.broadcasted_iota(jnp.int32, log_f.shape, 1)
        c = log_f
        d = 1
        while d < tm:
            c = c + jnp.where(lane >= d, pltpu.roll(c, d, axis=1), 0.0)
            d *= 2

        @pl.when(i % tiles_per_seq == 0)
        def _():
            carry_scr[...] = jnp.zeros_like(carry_scr)

        c = c + carry_scr[...][:, :1]
        carry_scr[...] = jnp.broadcast_to(c[:, tm - 1:tm], carry_scr.shape)
        terms = [t.astype(F32) for t in _bf16_terms(c[:FOX_HEADS] * LOG2E)]
        pad = jnp.zeros((LANES - N_TERMS * FOX_HEADS, tm), F32)
        caux_t = jnp.concatenate(terms + [pad], axis=0).astype(BF16)
        caux_ref[...] = _transpose_to_rows(caux_t).astype(caux_ref.dtype)

    proj_ref[...] = jnp.dot(h_scr[...], w_ref[...],
                            preferred_element_type=F32).astype(proj_ref.dtype)


def _inproj(x2d, ln_g, w_packed, wff_t, bf_col, *, seq):
    tokens = x2d.shape[0]
    tm, tn = 1024, 1024
    tiles_per_seq = seq // tm
    grid = (tokens // tm, PACKED_COLS // tn)
    return pl.pallas_call(
        functools.partial(_inproj_kernel, tiles_per_seq=tiles_per_seq),
        grid=grid,
        in_specs=[
            pl.BlockSpec((tm, D_MODEL), lambda i, j: (i, 0)),
            pl.BlockSpec((1, D_MODEL), lambda i, j: (0, 0)),
            pl.BlockSpec((D_MODEL, tn), lambda i, j: (0, j)),
            pl.BlockSpec((BF16_SUBLANES, D_MODEL), lambda i, j: (0, 0)),
            pl.BlockSpec((BF16_SUBLANES, 1), lambda i, j: (0, 0)),
        ],
        out_specs=[
            pl.BlockSpec((tm, tn), lambda i, j: (i, j)),
            pl.BlockSpec((tm, LANES), lambda i, j: (i, 0)),
        ],
        out_shape=[
            jax.ShapeDtypeStruct((tokens, PACKED_COLS), BF16),
            jax.ShapeDtypeStruct((tokens, LANES), BF16),
        ],
        scratch_shapes=[
            pltpu.VMEM((tm, D_MODEL), BF16),
            pltpu.VMEM((BF16_SUBLANES, LANES), F32),
        ],
        compiler_params=pltpu.CompilerParams(
            dimension_semantics=("arbitrary", "arbitrary"),
            vmem_limit_bytes=VMEM_LIMIT_BYTES),
        name="inproj",
    )(x2d, ln_g, w_packed, wff_t, bf_col)


def _mla_up_kernel(cq_ref, ckv_ref, krp_ref, cs_ref, gq_ref, gkv_ref, wq_ref, wkv_ref,
                   q_ref, k_ref, v_ref):
    lane = lax.broadcasted_iota(jnp.int32, (1, LANES), 1)
    cs = cs_ref[...]
    scale = MLA_QK ** -0.5 * LOG2E

    def rope_pair(a):
        t = a * cs
        return t + pltpu.roll(t, LANES // 2, axis=1)

    nq = _rms(cq_ref[...].astype(F32), gq_ref[...]).astype(BF16)
    qa = jnp.dot(nq, wq_ref[...], preferred_element_type=F32)
    for h in range(MLA_HEADS):
        base = h * QK_DEPTH
        q_ref[:, base:base + LANES] = (qa[:, base:base + LANES] * scale).astype(q_ref.dtype)
        rp = rope_pair(qa[:, base + LANES:base + QK_DEPTH])
        rp = jnp.where(lane < MLA_ROPE, rp, 0.0)
        q_ref[:, base + LANES:base + QK_DEPTH] = (rp * scale).astype(q_ref.dtype)

    nkv = _rms(ckv_ref[...].astype(F32), gkv_ref[...]).astype(BF16)
    kv = jnp.dot(nkv, wkv_ref[...], preferred_element_type=F32)
    kr = rope_pair(krp_ref[...].astype(F32)).astype(k_ref.dtype)
    for h in range(MLA_HEADS):
        base = h * QK_DEPTH
        k_ref[:, base:base + LANES] = kv[:, h * LANES:(h + 1) * LANES].astype(k_ref.dtype)
        k_ref[:, base + LANES:base + QK_DEPTH] = kr
    v_ref[...] = kv[:, MLA_HEADS * LANES:].astype(v_ref.dtype)


def _mla_up(proj, cs, g_q, g_kv, wq_packed, wkv_packed, *, seq):
    tokens = proj.shape[0]
    tm = 1024
    tiles_per_seq = seq // tm
    qk_cols = MLA_HEADS * QK_DEPTH
    return pl.pallas_call(
        _mla_up_kernel,
        grid=(tokens // tm,),
        in_specs=[
            pl.BlockSpec((tm, MLA_Q_RANK), lambda i: (i, COL_CQ * LANES // MLA_Q_RANK)),
            pl.BlockSpec((tm, LANES), lambda i: (i, COL_CKV)),
            pl.BlockSpec((tm, LANES), lambda i: (i, COL_KR)),
            pl.BlockSpec((tm, LANES), lambda i: (i % tiles_per_seq, 0)),
            pl.BlockSpec((1, MLA_Q_RANK), lambda i: (0, 0)),
            pl.BlockSpec((1, MLA_KV_RANK), lambda i: (0, 0)),
            pl.BlockSpec((MLA_Q_RANK, qk_cols), lambda i: (0, 0)),
            pl.BlockSpec((MLA_KV_RANK, 2 * MLA_WIDTH), lambda i: (0, 0)),
        ],
        out_specs=[
            pl.BlockSpec((tm, qk_cols), lambda i: (i, 0)),
            pl.BlockSpec((tm, qk_cols), lambda i: (i, 0)),
            pl.BlockSpec((tm, MLA_WIDTH), lambda i: (i, 0)),
        ],
        out_shape=[
            jax.ShapeDtypeStruct((tokens, qk_cols), BF16),
            jax.ShapeDtypeStruct((tokens, qk_cols), BF16),
            jax.ShapeDtypeStruct((tokens, MLA_WIDTH), BF16),
        ],
        compiler_params=pltpu.CompilerParams(
            dimension_semantics=("arbitrary",),
            vmem_limit_bytes=VMEM_LIMIT_BYTES),
        name="mla_up",
    )(proj, proj, proj, cs, g_q, g_kv, wq_packed, wkv_packed)


def _flash_update(s_ref, mx_ref, v, m_ref, l_ref, acc_ref, mask):
    if mask is None:
        s = s_ref[...]
        chunk_max = mx_ref[...]
    else:
        s = jnp.where(mask, s_ref[...], NEG)
        chunk_max = _chunk_max(s)
    m_prev = m_ref[...]
    m_next = jnp.maximum(m_prev, jnp.max(chunk_max, axis=1, keepdims=True))
    alpha = jnp.exp2(m_prev - m_next)
    p = [jnp.exp2(s[:, c:c + LANES] - m_next) for c in range(0, s.shape[1], LANES)]
    l_ref[...] = alpha * l_ref[...] + functools.reduce(lambda a, b: a + b, p)
    pv = jnp.dot(jnp.concatenate(p, axis=1).astype(v.dtype), v, preferred_element_type=F32)
    acc_ref[...] = alpha * acc_ref[...] + pv
    m_ref[...] = m_next


def _chunk_max(s):
    return functools.reduce(
        jnp.maximum, [s[:, c:c + LANES] for c in range(0, s.shape[1], LANES)])


def _flash_sweep(q_of, k_tile, v_tile, s_bufs, mx_bufs, m_scr, l_scr, acc_scr, *, n_sub, tile):
    qi = pl.program_id(2)
    m_scr[...] = jnp.full(m_scr.shape, NEG, F32)
    l_scr[...] = jnp.zeros(l_scr.shape, F32)
    acc_scr[...] = jnp.zeros(acc_scr.shape, F32)
    row = lax.broadcasted_iota(jnp.int32, (tile, tile), 0)
    col = lax.broadcasted_iota(jnp.int32, (tile, tile), 1)
    causal = col <= row

    def logits_into(buf, kt):
        start = pl.multiple_of(kt * tile, tile)
        for sub in range(n_sub):
            s = _nt_dot(q_of(sub), k_tile(sub, start))
            s_bufs[buf][sub] = s
            mx_bufs[buf][sub] = _chunk_max(s)

    def step(read, write, kt, mask):
        if write is not None:
            logits_into(write, kt + 1)
        start = pl.multiple_of(kt * tile, tile)
        for sub in range(n_sub):
            _flash_update(s_bufs[read].at[sub], mx_bufs[read].at[sub], v_tile(sub, start),
                          m_scr.at[sub], l_scr.at[sub], acc_scr.at[sub], mask)

    logits_into(0, 0)

    def pair(i, carry):
        step(0, 1, 2 * i, None)
        step(1, 0, 2 * i + 1, None)
        return carry

    lax.fori_loop(0, qi // 2, pair, 0)

    @pl.when(qi % 2 == 1)
    def _():
        step(0, 1, qi - 1, None)
        step(1, None, qi, causal)

    @pl.when(qi % 2 == 0)
    def _():
        step(0, None, qi, causal)


def _flash_result(l_scr, acc_scr, sub):
    return acc_scr[sub] / jnp.sum(l_scr[sub], axis=1, keepdims=True)


def _merge_head_pair(l_scr, acc_scr):
    lane = lax.broadcasted_iota(jnp.int32, (1, LANES), 1)
    return jnp.where(lane < HEAD_DIM, _flash_result(l_scr, acc_scr, 0),
                     _flash_result(l_scr, acc_scr, 1))


def _head_lane_mask(sub):
    lane = lax.broadcasted_iota(jnp.int32, (1, LANES), 1)
    return (lane < HEAD_DIM) if sub == 0 else (lane >= HEAD_DIM)


def _fox_kernel(q_ref, k_ref, v_ref, caux_ref, o_ref,
                kx_scr, qx_scr, s_a, s_b, mx_a, mx_b, m_scr, l_scr, acc_scr, *, tile):
    g = pl.program_id(1)

    @pl.when(pl.program_id(2) == 0)
    def _():
        kx_scr[:, :LANES] = k_ref[0]
        kx_scr[:, LANES:] = caux_ref[0]

    q = (q_ref[0].astype(F32) * (HEAD_DIM ** -0.5 * LOG2E)).astype(q_ref.dtype)
    lane = lax.broadcasted_iota(jnp.int32, (tile, LANES), 1)
    for sub in range(2):
        head = 2 * g + sub
        qx_scr[sub, :, :LANES] = jnp.where(_head_lane_mask(sub), q, jnp.zeros_like(q))
        picks = functools.reduce(
            lambda a, b: a | b, [lane == head + t * FOX_HEADS for t in range(N_TERMS)])
        qx_scr[sub, :, LANES:] = jnp.where(picks, -1.0, 0.0).astype(qx_scr.dtype)

    _flash_sweep(lambda sub: qx_scr[sub],
                 lambda sub, start: kx_scr[pl.ds(start, tile), :],
                 lambda sub, start: v_ref[0, pl.ds(start, tile), :],
                 (s_a, s_b), (mx_a, mx_b), m_scr, l_scr, acc_scr, n_sub=2, tile=tile)
    o_ref[0] = _merge_head_pair(l_scr, acc_scr).astype(o_ref.dtype)


def _mla_kernel(q_ref, k_ref, v_ref, o_ref, s_a, s_b, mx_a, mx_b, m_scr, l_scr, acc_scr, *, tile):
    _flash_sweep(lambda sub: q_ref[0, :, sub * QK_DEPTH:(sub + 1) * QK_DEPTH],
                 lambda sub, start: k_ref[0, pl.ds(start, tile), sub * QK_DEPTH:(sub + 1) * QK_DEPTH],
                 lambda sub, start: v_ref[0, pl.ds(start, tile), sub * LANES:(sub + 1) * LANES],
                 (s_a, s_b), (mx_a, mx_b), m_scr, l_scr, acc_scr, n_sub=2, tile=tile)
    for sub in range(2):
        o_ref[0, :, sub * LANES:(sub + 1) * LANES] = (
            _flash_result(l_scr, acc_scr, sub).astype(o_ref.dtype))


def _moba_kernel(q_ref, k_ref, v_ref, kaux_ref, slope_ref, o_ref,
                 kx_scr, kmean_scr, qx_scr, s_a, s_b, mx_a, mx_b, m_scr, l_scr, acc_scr,
                 *, tile, n_blocks):
    blk = MOBA_BLOCK
    g = pl.program_id(1)
    qi = pl.program_id(2)

    @pl.when(qi == 0)
    def _():
        k = k_ref[0]
        kx_scr[:, :LANES] = k
        kx_scr[:, LANES:] = kaux_ref[...]
        kmean = jnp.sum(k.astype(F32).reshape(n_blocks, blk, LANES), axis=1) * (1.0 / blk)
        for t, term in enumerate(_bf16_terms(kmean)):
            kmean_scr[t] = term

    q_start = qi * tile
    q_raw = q_ref[0]
    q_scaled = (q_raw.astype(F32) * (HEAD_DIM ** -0.5 * LOG2E)).astype(q_raw.dtype)
    blk_id = lax.broadcasted_iota(jnp.int32, (n_blocks, tile), 0)
    own = jnp.right_shift(q_start + lax.broadcasted_iota(jnp.int32, (1, tile), 1),
                          blk.bit_length() - 1)
    row_id = lax.broadcasted_iota(jnp.int32, (LANES - n_blocks, tile), 0)
    for sub in range(2):
        head_mask = _head_lane_mask(sub)
        q_head = jnp.where(head_mask, q_raw, jnp.zeros_like(q_raw))
        gate = functools.reduce(
            lambda a, b: a + b, [_nt_dot(kmean_scr[t], q_head) for t in range(N_TERMS)])
        rank = jnp.zeros((n_blocks, tile), jnp.int32)
        for m in range(n_blocks - 1):
            gm = gate[m:m + 1, :]
            ahead = ((gm > gate) | ((gm == gate) & (m < blk_id))) & (m < own)
            rank = rank + ahead.astype(jnp.int32)
        allowed = ((rank < MOBA_TOPK) & (blk_id < own)) | (blk_id == own)
        sel_t = jnp.where(allowed, 0.0, NEG)
        alibi_t = jnp.zeros(row_id.shape, F32)
        for t in range(N_TERMS):
            term = slope_ref[pl.ds(t * MOBA_HEADS + 2 * g + sub, 1), :]
            term = jnp.concatenate([term] * (tile // LANES), axis=1)
            alibi_t = jnp.where(row_id == 2 * t, term * float(blk),
                                jnp.where(row_id == 2 * t + 1, term, alibi_t))
        aux_t = jnp.concatenate([sel_t, alibi_t], axis=0).astype(BF16)
        qx_scr[sub, :, :LANES] = jnp.where(head_mask, q_scaled, jnp.zeros_like(q_scaled))
        qx_scr[sub, :, LANES:] = _transpose_to_rows(aux_t).astype(qx_scr.dtype)

    _flash_sweep(lambda sub: qx_scr[sub],
                 lambda sub, start: kx_scr[pl.ds(start, tile), :],
                 lambda sub, start: v_ref[0, pl.ds(start, tile), :],
                 (s_a, s_b), (mx_a, mx_b), m_scr, l_scr, acc_scr, n_sub=2, tile=tile)
    o_ref[0] = _merge_head_pair(l_scr, acc_scr).astype(o_ref.dtype)


def _attention(kind, q_arr, k_arr, v_arr, extras, *, batch, seq, q_col, k_col, v_col):
    tile = ATTN_TILE
    n_sub = 2
    mla = kind == "mla"
    groups = (MLA_HEADS if mla else FOX_HEADS) // n_sub
    dk = n_sub * QK_DEPTH if mla else LANES
    dv = n_sub * LANES if mla else LANES
    in_specs = [
        pl.BlockSpec((1, tile, dk), lambda b, g, qi: (b, qi, q_col + g)),
        pl.BlockSpec((1, seq, dk), lambda b, g, qi: (b, 0, k_col + g)),
        pl.BlockSpec((1, seq, dv), lambda b, g, qi: (b, 0, v_col + g)),
    ]
    sweep_scr = ([pltpu.VMEM((n_sub, tile, tile), F32)] * 2
                 + [pltpu.VMEM((n_sub, tile, LANES), F32)] * 5)
    operand_scr = [pltpu.VMEM((seq, QK_DEPTH), BF16), pltpu.VMEM((n_sub, tile, QK_DEPTH), BF16)]
    if kind == "fox":
        body = functools.partial(_fox_kernel, tile=tile)
        in_specs.append(pl.BlockSpec((1, seq, LANES), lambda b, g, qi: (b, 0, 0)))
        scratch = operand_scr + sweep_scr
    elif mla:
        body = functools.partial(_mla_kernel, tile=tile)
        scratch = sweep_scr
    else:
        n_blocks = seq // MOBA_BLOCK
        body = functools.partial(_moba_kernel, tile=tile, n_blocks=n_blocks)
        in_specs.append(pl.BlockSpec((seq, LANES), lambda b, g, qi: (0, 0)))
        in_specs.append(pl.BlockSpec((N_TERMS * MOBA_HEADS, LANES), lambda b, g, qi: (0, 0)))
        scratch = ([operand_scr[0], pltpu.VMEM((N_TERMS, n_blocks, LANES), BF16), operand_scr[1]]
                   + sweep_scr)
    return pl.pallas_call(
        body,
        grid=(batch, groups, seq // tile),
        in_specs=in_specs,
        out_specs=pl.BlockSpec((1, tile, dv), lambda b, g, qi: (b, qi, g)),
        out_shape=jax.ShapeDtypeStruct((batch, seq, groups * dv), BF16),
        scratch_shapes=scratch,
        compiler_params=pltpu.CompilerParams(
            dimension_semantics=("arbitrary", "arbitrary", "arbitrary"),
            vmem_limit_bytes=VMEM_LIMIT_BYTES),
        name=kind + "_attn",
    )(q_arr, k_arr, v_arr, *extras)


def _combine_kernel(ya_ref, yb_ref, yc_ref, za_ref, zb_ref, zc_ref, x_ref, og_ref, w_ref, fg_ref,
                    o_ref, *, final):
    acc = x_ref[...]
    groups = ((ya_ref, za_ref), (yb_ref, zb_ref), (yc_ref, zc_ref))
    for idx, (y_ref, z_ref) in enumerate(groups):
        lo, hi = idx * FOX_WIDTH, (idx + 1) * FOX_WIDTH
        z = z_ref[...].astype(F32)
        gated = _rms(y_ref[...].astype(F32), og_ref[:, lo:hi]) * (z / (1.0 + jnp.exp(-z)))
        acc = acc + jnp.dot(gated.astype(BF16), w_ref[lo:hi, :], preferred_element_type=F32)
    if final:
        acc = _rms(acc, fg_ref[...])
    o_ref[...] = acc


def _combine(ya, yb, yc, proj, x2d, out_g, w_out, final_g, *, final):
    tokens = x2d.shape[0]
    tm = 512
    width = FOX_WIDTH
    zcol = lambda c: c * LANES // width
    y_spec = pl.BlockSpec((tm, width), lambda i: (i, 0))
    return pl.pallas_call(
        functools.partial(_combine_kernel, final=final),
        grid=(tokens // tm,),
        in_specs=[
            y_spec, y_spec, y_spec,
            pl.BlockSpec((tm, width), lambda i: (i, zcol(COL_FZ))),
            pl.BlockSpec((tm, width), lambda i: (i, zcol(COL_MZ))),
            pl.BlockSpec((tm, width), lambda i: (i, zcol(COL_BZ))),
            pl.BlockSpec((tm, D_MODEL), lambda i: (i, 0)),
            pl.BlockSpec((1, D_MIX), lambda i: (0, 0)),
            pl.BlockSpec((D_MIX, D_MODEL), lambda i: (0, 0)),
            pl.BlockSpec((1, D_MODEL), lambda i: (0, 0)),
        ],
        out_specs=pl.BlockSpec((tm, D_MODEL), lambda i: (i, 0)),
        out_shape=jax.ShapeDtypeStruct((tokens, D_MODEL), F32),
        compiler_params=pltpu.CompilerParams(
            dimension_semantics=("arbitrary",),
            vmem_limit_bytes=VMEM_LIMIT_BYTES),
        name="combine",
    )(ya, yb, yc, proj, proj, proj, x2d, out_g, w_out, final_g)


def _rot_cols(w):
    half = MLA_ROPE // 2
    return jnp.concatenate([-w[..., half:], w[..., :half]], axis=-1)


def _pack_layer(w_in, w_uq, w_ukv):
    cuts = np.cumsum(IN_SIZES)[:-1].tolist()
    fq, fk, fv, ff, fz, cq, ckv, kr, mz, bq, bk, bv, bz = jnp.split(w_in, cuts, axis=1)
    w_packed = jnp.concatenate(
        [fq, fk, fv, fz, bq, bk, bv, bz, mz, cq, ckv, kr, _rot_cols(kr)], axis=1).astype(BF16)
    wff_t = jnp.zeros((BF16_SUBLANES, D_MODEL), F32).at[:FOX_HEADS].set(ff.T).astype(BF16)
    wq = w_uq.reshape(MLA_Q_RANK, MLA_HEADS, MLA_QK)
    nope, rope = wq[..., :MLA_NOPE], wq[..., MLA_NOPE:]
    wq_packed = jnp.concatenate([nope, rope, _rot_cols(rope)], axis=-1)
    wq_packed = wq_packed.reshape(MLA_Q_RANK, MLA_HEADS * QK_DEPTH).astype(BF16)
    wkv = w_ukv.reshape(MLA_KV_RANK, MLA_HEADS, MLA_NOPE + MLA_V)
    wkv_packed = jnp.concatenate(
        [wkv[..., :MLA_NOPE].reshape(MLA_KV_RANK, MLA_WIDTH),
         wkv[..., MLA_NOPE:].reshape(MLA_KV_RANK, MLA_WIDTH)], axis=1).astype(BF16)
    return w_packed, wff_t, wq_packed, wkv_packed


def _moba_key_aux(seq):
    pos = np.arange(seq)
    aux = np.zeros((seq, LANES), np.float32)
    n_blocks = seq // MOBA_BLOCK
    aux[pos, pos // MOBA_BLOCK] = 1.0
    for t in range(N_TERMS):
        aux[:, n_blocks + 2 * t] = pos // MOBA_BLOCK
        aux[:, n_blocks + 2 * t + 1] = pos % MOBA_BLOCK
    return jnp.asarray(aux, BF16)


def kernel(x, ln_g, w_in, fox_b_f, mla_q_g, mla_w_uq, mla_kv_g, mla_w_ukv, out_g, w_out, final_g):
    batch, seq, _ = x.shape
    depth = ln_g.shape[0]
    tokens = batch * seq

    inv = ROPE_THETA ** (-jnp.arange(0, MLA_ROPE, 2, dtype=F32) / MLA_ROPE)
    ang = jnp.arange(seq, dtype=F32)[:, None] * inv[None, :]
    ang = jnp.concatenate([ang, ang], axis=-1)
    cs = jnp.concatenate([jnp.cos(ang), jnp.sin(ang)], axis=-1)
    slopes = 2.0 ** (-8.0 * jnp.arange(1, MOBA_HEADS + 1, dtype=F32) / MOBA_HEADS)
    slope_terms = jnp.concatenate([t.astype(F32) for t in _bf16_terms(slopes * LOG2E)])
    slope_terms = jnp.broadcast_to(slope_terms[:, None], (N_TERMS * MOBA_HEADS, LANES))
    moba_kaux = _moba_key_aux(seq)

    x2d = x.reshape(tokens, D_MODEL)
    for l in range(depth):
        w_packed, wff_t, wq_packed, wkv_packed = _pack_layer(w_in[l], mla_w_uq[l], mla_w_ukv[l])
        bf_col = jnp.zeros((BF16_SUBLANES, 1), F32).at[:FOX_HEADS, 0].set(fox_b_f[l])
        proj, caux = _inproj(x2d, ln_g[l][None, :], w_packed, wff_t, bf_col, seq=seq)
        q_cat, k_cat, v_mla = _mla_up(proj, cs, mla_q_g[l][None, :], mla_kv_g[l][None, :],
                                      wq_packed, wkv_packed, seq=seq)
        proj3 = proj.reshape(batch, seq, PACKED_COLS)
        ya = _attention("fox", proj3, proj3, proj3, [caux.reshape(batch, seq, LANES)],
                        batch=batch, seq=seq, q_col=COL_FQ, k_col=COL_FK, v_col=COL_FV)
        yb = _attention("mla", q_cat.reshape(batch, seq, -1), k_cat.reshape(batch, seq, -1),
                        v_mla.reshape(batch, seq, -1), [], batch=batch, seq=seq,
                        q_col=0, k_col=0, v_col=0)
        yc = _attention("moba", proj3, proj3, proj3, [moba_kaux, slope_terms],
                        batch=batch, seq=seq, q_col=COL_BQ, k_col=COL_BK, v_col=COL_BV)
        x2d = _combine(ya.reshape(tokens, -1), yb.reshape(tokens, -1), yc.reshape(tokens, -1),
                       proj, x2d, out_g[l][None, :], w_out[l].astype(BF16), final_g[None, :],
                       final=(l == depth - 1))
    return x2d.reshape(batch, seq, D_MODEL)
```

```python
import functools
from typing import Any, NamedTuple

import jax
import jax.numpy as jnp
import numpy as np
from jax import lax
from jax.experimental import pallas as pl
from jax.experimental.pallas import tpu as pltpu

D_MODEL = 1024
HEAD_DIM = 64
FOX_HEADS = 8
FOX_WIDTH = FOX_HEADS * HEAD_DIM
MLA_HEADS = 4
MLA_NOPE = 128
MLA_ROPE = 64
MLA_QK = MLA_NOPE + MLA_ROPE
MLA_V = 128
MLA_Q_RANK = 256
MLA_KV_RANK = 128
MLA_WIDTH = MLA_HEADS * MLA_V
MOBA_HEADS = 8
MOBA_WIDTH = MOBA_HEADS * HEAD_DIM
MOBA_BLOCK = 256
MOBA_TOPK = 3
D_MIX = FOX_WIDTH + MLA_WIDTH + MOBA_WIDTH
ROPE_THETA = 10000.0
EPS = 1e-6
IN_SIZES = (FOX_WIDTH, FOX_WIDTH, FOX_WIDTH, FOX_HEADS, FOX_WIDTH,
            MLA_Q_RANK, MLA_KV_RANK, MLA_ROPE, MLA_WIDTH,
            MOBA_WIDTH, MOBA_WIDTH, MOBA_WIDTH, MOBA_WIDTH)

LANES = 128
SUBLANES = 8
BF16_SUBLANES = 16
MXU_DEPTH = 256
VMEM_LIMIT_BYTES = 56 * 1024 * 1024

COL_FQ, COL_FK, COL_FV, COL_FZ = 0, 4, 8, 12
COL_BQ, COL_BK, COL_BV, COL_BZ = 16, 20, 24, 28
COL_MZ, COL_CQ, COL_CKV, COL_KR = 32, 36, 38, 39
PACKED_COLS = 40 * LANES

QK_DEPTH = MXU_DEPTH
N_TERMS = 3
NEG = -1e30
LOG2E = 1.4426950408889634
ATTN_TILE = 512
SWEEP_UNROLL = 4

F32 = jnp.float32
BF16 = jnp.bfloat16


def _rms(x, g):
    return x * lax.rsqrt(jnp.mean(x * x, axis=-1, keepdims=True) + EPS) * g


def _nt_dot(a, b):
    return lax.dot_general(a, b, (((1,), (1,)), ((), ())), preferred_element_type=F32)


def _bf16_terms(x):
    hi = x.astype(BF16)
    mid = (x - hi.astype(F32)).astype(BF16)
    lo = (x - hi.astype(F32) - mid.astype(F32)).astype(BF16)
    return hi, mid, lo


def _transpose_to_rows(cols_t):
    n = cols_t.shape[1]
    blk = MXU_DEPTH
    eye = (lax.broadcasted_iota(jnp.int32, (blk, blk), 0)
           == lax.broadcasted_iota(jnp.int32, (blk, blk), 1)).astype(BF16)
    return jnp.concatenate(
        [_nt_dot(eye, cols_t[:, s:s + blk]) for s in range(0, n, blk)], axis=0)


def _inproj_kernel(x_ref, g_ref, w_ref, wff_ref, bf_ref, proj_ref, caux_ref, h_scr, carry_scr,
                   *, tiles_per_seq):
    i = pl.program_id(0)
    j = pl.program_id(1)

    @pl.when(j == 0)
    def _():
        h = _rms(x_ref[...], g_ref[...]).astype(BF16)
        h_scr[...] = h
        logit = _nt_dot(wff_ref[...], h) + bf_ref[...]
        log_f = jnp.minimum(logit, 0.0) - jnp.log1p(jnp.exp(-jnp.abs(logit)))
        tm = log_f.shape[1]
        lane = lax.broadcasted_iota(jnp.int32, log_f.shape, 1)
        c = log_f
        d = 1
        while d < tm:
            c = c + jnp.where(lane >= d, pltpu.roll(c, d, axis=1), 0.0)
            d *= 2

        @pl.when(i % tiles_per_seq == 0)
        def _():
            carry_scr[...] = jnp.zeros_like(carry_scr)

        c = c + carry_scr[...][:, :1]
        carry_scr[...] = jnp.broadcast_to(c[:, tm - 1:tm], carry_scr.shape)
        terms = [t.astype(F32) for t in _bf16_terms(c[:FOX_HEADS] * LOG2E)]
        pad = jnp.zeros((LANES - N_TERMS * FOX_HEADS, tm), F32)
        caux_t = jnp.concatenate(terms + [pad], axis=0).astype(BF16)
        caux_ref[...] = _transpose_to_rows(caux_t).astype(caux_ref.dtype)

    proj_ref[...] = jnp.dot(h_scr[...], w_ref[...],
                            preferred_element_type=F32).astype(proj_ref.dtype)


def _inproj(x2d, ln_g, w_packed, wff_t, bf_col, *, seq):
    tokens = x2d.shape[0]
    tm, tn = 1024, 1024
    tiles_per_seq = seq // tm
    grid = (tokens // tm, PACKED_COLS // tn)
    return pl.pallas_call(
        functools.partial(_inproj_kernel, tiles_per_seq=tiles_per_seq),
        grid=grid,
        in_specs=[
            pl.BlockSpec((tm, D_MODEL), lambda i, j: (i, 0)),
            pl.BlockSpec((1, D_MODEL), lambda i, j: (0, 0)),
            pl.BlockSpec((D_MODEL, tn), lambda i, j: (0, j)),
            pl.BlockSpec((BF16_SUBLANES, D_MODEL), lambda i, j: (0, 0)),
            pl.BlockSpec((BF16_SUBLANES, 1), lambda i, j: (0, 0)),
        ],
        out_specs=[
            pl.BlockSpec((tm, tn), lambda i, j: (i, j)),
            pl.BlockSpec((tm, LANES), lambda i, j: (i, 0)),
        ],
        out_shape=[
            jax.ShapeDtypeStruct((tokens, PACKED_COLS), BF16),
            jax.ShapeDtypeStruct((tokens, LANES), BF16),
        ],
        scratch_shapes=[
            pltpu.VMEM((tm, D_MODEL), BF16),
            pltpu.VMEM((BF16_SUBLANES, LANES), F32),
        ],
        compiler_params=pltpu.CompilerParams(
            dimension_semantics=("arbitrary", "arbitrary"),
            vmem_limit_bytes=VMEM_LIMIT_BYTES),
        name="inproj",
    )(x2d, ln_g, w_packed, wff_t, bf_col)


def _mla_up_kernel(cq_ref, ckv_ref, krp_ref, cs_ref, gq_ref, gkv_ref, wq_ref, wkv_ref,
                   q_ref, k_ref, v_ref):
    lane = lax.broadcasted_iota(jnp.int32, (1, LANES), 1)
    cs = cs_ref[...]
    scale = MLA_QK ** -0.5 * LOG2E

    def rope_pair(a):
        t = a * cs
        return t + pltpu.roll(t, LANES // 2, axis=1)

    nq = _rms(cq_ref[...].astype(F32), gq_ref[...]).astype(BF16)
    qa = jnp.dot(nq, wq_ref[...], preferred_element_type=F32)
    for h in range(MLA_HEADS):
        base = h * QK_DEPTH
        q_ref[:, base:base + LANES] = (qa[:, base:base + LANES] * scale).astype(q_ref.dtype)
        rp = rope_pair(qa[:, base + LANES:base + QK_DEPTH])
        rp = jnp.where(lane < MLA_ROPE, rp, 0.0)
        q_ref[:, base + LANES:base + QK_DEPTH] = (rp * scale).astype(q_ref.dtype)

    nkv = _rms(ckv_ref[...].astype(F32), gkv_ref[...]).astype(BF16)
    kv = jnp.dot(nkv, wkv_ref[...], preferred_element_type=F32)
    kr = rope_pair(krp_ref[...].astype(F32)).astype(k_ref.dtype)
    for h in range(MLA_HEADS):
        base = h * QK_DEPTH
        k_ref[:, base:base + LANES] = kv[:, h * LANES:(h + 1) * LANES].astype(k_ref.dtype)
        k_ref[:, base + LANES:base + QK_DEPTH] = kr
    v_ref[...] = kv[:, MLA_HEADS * LANES:].astype(v_ref.dtype)


def _mla_up(proj, cs, g_q, g_kv, wq_packed, wkv_packed, *, seq):
    tokens = proj.shape[0]
    tm = 1024
    tiles_per_seq = seq // tm
    qk_cols = MLA_HEADS * QK_DEPTH
    return pl.pallas_call(
        _mla_up_kernel,
        grid=(tokens // tm,),
        in_specs=[
            pl.BlockSpec((tm, MLA_Q_RANK), lambda i: (i, COL_CQ * LANES // MLA_Q_RANK)),
            pl.BlockSpec((tm, LANES), lambda i: (i, COL_CKV)),
            pl.BlockSpec((tm, LANES), lambda i: (i, COL_KR)),
            pl.BlockSpec((tm, LANES), lambda i: (i % tiles_per_seq, 0)),
            pl.BlockSpec((1, MLA_Q_RANK), lambda i: (0, 0)),
            pl.BlockSpec((1, MLA_KV_RANK), lambda i: (0, 0)),
            pl.BlockSpec((MLA_Q_RANK, qk_cols), lambda i: (0, 0)),
            pl.BlockSpec((MLA_KV_RANK, 2 * MLA_WIDTH), lambda i: (0, 0)),
        ],
        out_specs=[
            pl.BlockSpec((tm, qk_cols), lambda i: (i, 0)),
            pl.BlockSpec((tm, qk_cols), lambda i: (i, 0)),
            pl.BlockSpec((tm, MLA_WIDTH), lambda i: (i, 0)),
        ],
        out_shape=[
            jax.ShapeDtypeStruct((tokens, qk_cols), BF16),
            jax.ShapeDtypeStruct((tokens, qk_cols), BF16),
            jax.ShapeDtypeStruct((tokens, MLA_WIDTH), BF16),
        ],
        compiler_params=pltpu.CompilerParams(
            dimension_semantics=("arbitrary",),
            vmem_limit_bytes=VMEM_LIMIT_BYTES),
        name="mla_up",
    )(proj, proj, proj, cs, g_q, g_kv, wq_packed, wkv_packed)


class _SweepScratch(NamedTuple):
    s: Any
    mx: Any
    p: Any
    alpha: Any
    m: Any
    l: Any
    acc: Any

    @classmethod
    def of(cls, refs):
        s0, s1, mx0, mx1, p0, p1, a0, a1, m, l, acc = refs
        return cls((s0, s1), (mx0, mx1), (p0, p1), (a0, a1), m, l, acc)

    @staticmethod
    def shapes(n_sub, tile, n_q):
        big = lambda dt: pltpu.VMEM((n_sub, tile, tile), dt)
        small = pltpu.VMEM((n_sub, tile, LANES), F32)
        state = pltpu.VMEM((n_q + 1, n_sub, tile, LANES), F32)
        return [big(F32)] * 2 + [small] * 2 + [big(BF16)] * 2 + [small] * 2 + [state] * 3


def _sweep_items(n_q):
    diag = [(i, i) for i in range(n_q)]
    past = [(i, t) for i in range(n_q) for t in range(i)]
    items = diag + [(n_q, 0)] + past + [(n_q, 0)]
    return (jnp.asarray([i for i, _ in items], jnp.int32),
            jnp.asarray([t for _, t in items], jnp.int32))


def _chunk_max(s):
    return functools.reduce(
        jnp.maximum, [s[:, c:c + LANES] for c in range(0, s.shape[1], LANES)])


def _flash_sweep(tabs, q_rows, k_tile, v_tile, scr, *, n_sub, tile, n_q):
    qi_tab, t_tab = tabs
    row = lax.broadcasted_iota(jnp.int32, (tile, tile), 0)
    col = lax.broadcasted_iota(jnp.int32, (tile, tile), 1)
    causal = col <= row

    def logits(j, buf):
        q_start = pl.multiple_of(qi_tab[j] * tile, tile)
        k_start = pl.multiple_of(t_tab[j] * tile, tile)
        for sub in range(n_sub):
            s = _nt_dot(q_rows(sub, q_start), k_tile(sub, k_start))
            scr.s[buf][sub] = s
            scr.mx[buf][sub] = _chunk_max(s)

    def softmax(j, buf, diag):
        slot = qi_tab[j]
        for sub in range(n_sub):
            if diag:
                s = jnp.where(causal, scr.s[buf][sub], NEG)
                m_next = jnp.broadcast_to(jnp.max(_chunk_max(s), axis=1, keepdims=True),
                                          (tile, LANES))
            else:
                s = scr.s[buf][sub]
                m_prev = scr.m[slot, sub]
                m_next = jnp.maximum(m_prev, jnp.max(scr.mx[buf][sub], axis=1, keepdims=True))
                alpha = jnp.exp2(m_prev - m_next)
                scr.alpha[buf][sub] = alpha
            p = [jnp.exp2(s[:, c:c + LANES] - m_next) for c in range(0, tile, LANES)]
            p_sum = functools.reduce(lambda a, b: a + b, p)
            scr.l[slot, sub] = p_sum if diag else alpha * scr.l[slot, sub] + p_sum
            scr.m[slot, sub] = m_next
            scr.p[buf][sub] = jnp.concatenate(p, axis=1).astype(BF16)

    def weighted_values(j, buf, diag):
        slot = qi_tab[j]
        k_start = pl.multiple_of(t_tab[j] * tile, tile)
        for sub in range(n_sub):
            pv = jnp.dot(scr.p[buf][sub], v_tile(sub, k_start), preferred_element_type=F32)
            scr.acc[slot, sub] = pv if diag else scr.alpha[buf][sub] * scr.acc[slot, sub] + pv

    def phase(base, n_items, diag):
        assert n_items % SWEEP_UNROLL == 0 and SWEEP_UNROLL % 2 == 0
        last = base + n_items - 1
        logits(base, 0)
        logits(base + 1, 1)
        softmax(base, 0, diag)

        def steps(i, carry):
            for u in range(SWEEP_UNROLL):
                j = base + 1 + SWEEP_UNROLL * i + u
                buf = (1 + u) % 2
                weighted_values(j - 1, 1 - buf, diag)
                logits(jnp.minimum(j + 1, last), 1 - buf)
                softmax(j, buf, diag)
            return carry

        lax.fori_loop(0, n_items // SWEEP_UNROLL, steps, 0)

    phase(0, n_q, True)
    phase(n_q + 1, n_q * (n_q - 1) // 2, False)


def _flash_result(scr, qi, sub):
    return scr.acc[qi, sub] / jnp.sum(scr.l[qi, sub], axis=1, keepdims=True)


def _head_lane_mask(sub):
    lane = lax.broadcasted_iota(jnp.int32, (1, LANES), 1)
    return (lane < HEAD_DIM) if sub == 0 else (lane >= HEAD_DIM)


def _store_head_pair(o_ref, scr, *, tile, n_q):
    for qi in range(n_q):
        merged = jnp.where(_head_lane_mask(0), _flash_result(scr, qi, 0), _flash_result(scr, qi, 1))
        o_ref[0, qi * tile:(qi + 1) * tile, :] = merged.astype(o_ref.dtype)


def _fox_kernel(qi_tab, t_tab, q_ref, k_ref, v_ref, caux_ref, o_ref, kx_scr, qx_scr, *sweep_refs,
                tile, n_q):
    g = pl.program_id(1)
    scr = _SweepScratch.of(sweep_refs)
    kx_scr[:, :LANES] = k_ref[0]
    kx_scr[:, LANES:] = caux_ref[0]
    q = (q_ref[0].astype(F32) * (HEAD_DIM ** -0.5 * LOG2E)).astype(q_ref.dtype)
    lane = lax.broadcasted_iota(jnp.int32, q.shape, 1)
    for sub in range(2):
        head = 2 * g + sub
        qx_scr[sub, :, :LANES] = jnp.where(_head_lane_mask(sub), q, jnp.zeros_like(q))
        picks = functools.reduce(
            lambda a, b: a | b, [lane == head + t * FOX_HEADS for t in range(N_TERMS)])
        qx_scr[sub, :, LANES:] = jnp.where(picks, -1.0, 0.0).astype(qx_scr.dtype)

    _flash_sweep((qi_tab, t_tab),
                 lambda sub, start: qx_scr[sub, pl.ds(start, tile), :],
                 lambda sub, start: kx_scr[pl.ds(start, tile), :],
                 lambda sub, start: v_ref[0, pl.ds(start, tile), :],
                 scr, n_sub=2, tile=tile, n_q=n_q)
    _store_head_pair(o_ref, scr, tile=tile, n_q=n_q)


def _mla_kernel(qi_tab, t_tab, q_ref, k_ref, v_ref, o_ref, *sweep_refs, tile, n_q):
    scr = _SweepScratch.of(sweep_refs)
    _flash_sweep((qi_tab, t_tab),
                 lambda sub, start: q_ref[0, pl.ds(start, tile), :],
                 lambda sub, start: k_ref[0, pl.ds(start, tile), :],
                 lambda sub, start: v_ref[0, pl.ds(start, tile), :],
                 scr, n_sub=1, tile=tile, n_q=n_q)
    for qi in range(n_q):
        o_ref[0, qi * tile:(qi + 1) * tile, :] = _flash_result(scr, qi, 0).astype(o_ref.dtype)


def _moba_kernel(qi_tab, t_tab, q_ref, k_ref, v_ref, kaux_ref, slope_ref, o_ref,
                 kx_scr, qx_scr, *sweep_refs, tile, n_q, n_blocks):
    blk = MOBA_BLOCK
    g = pl.program_id(1)
    scr = _SweepScratch.of(sweep_refs)
    seq = q_ref.shape[1]

    k = k_ref[0]
    kx_scr[:, :LANES] = k
    kx_scr[:, LANES:] = kaux_ref[...]
    kmean = jnp.sum(k.astype(F32).reshape(n_blocks, blk, LANES), axis=1) * (1.0 / blk)
    kmean_terms = _bf16_terms(kmean)

    q_raw = q_ref[0]
    q_scaled = (q_raw.astype(F32) * (HEAD_DIM ** -0.5 * LOG2E)).astype(q_raw.dtype)
    blk_id = lax.broadcasted_iota(jnp.int32, (n_blocks, seq), 0)
    own = jnp.right_shift(lax.broadcasted_iota(jnp.int32, (1, seq), 1), blk.bit_length() - 1)
    row_id = lax.broadcasted_iota(jnp.int32, (LANES - n_blocks, seq), 0)
    for sub in range(2):
        head_mask = _head_lane_mask(sub)
        q_head = jnp.where(head_mask, q_raw, jnp.zeros_like(q_raw))
        gate = functools.reduce(
            lambda a, b: a + b, [_nt_dot(term, q_head) for term in kmean_terms])
        rank = jnp.zeros((n_blocks, seq), jnp.int32)
        for m in range(n_blocks - 1):
            gm = gate[m:m + 1, :]
            ahead = ((gm > gate) | ((gm == gate) & (m < blk_id))) & (m < own)
            rank = rank + ahead.astype(jnp.int32)
        allowed = ((rank < MOBA_TOPK) & (blk_id < own)) | (blk_id == own)
        sel_t = jnp.where(allowed, 0.0, NEG)
        alibi_t = jnp.zeros(row_id.shape, F32)
        for t in range(N_TERMS):
            term = slope_ref[pl.ds(t * MOBA_HEADS + 2 * g + sub, 1), :]
            term = jnp.concatenate([term] * (seq // LANES), axis=1)
            alibi_t = jnp.where(row_id == 2 * t, term * float(blk),
                                jnp.where(row_id == 2 * t + 1, term, alibi_t))
        aux_t = jnp.concatenate([sel_t, alibi_t], axis=0).astype(BF16)
        qx_scr[sub, :, :LANES] = jnp.where(head_mask, q_scaled, jnp.zeros_like(q_scaled))
        qx_scr[sub, :, LANES:] = _transpose_to_rows(aux_t).astype(qx_scr.dtype)

    _flash_sweep((qi_tab, t_tab),
                 lambda sub, start: qx_scr[sub, pl.ds(start, tile), :],
                 lambda sub, start: kx_scr[pl.ds(start, tile), :],
                 lambda sub, start: v_ref[0, pl.ds(start, tile), :],
                 scr, n_sub=2, tile=tile, n_q=n_q)
    _store_head_pair(o_ref, scr, tile=tile, n_q=n_q)


def _attention(kind, q_arr, k_arr, v_arr, extras, *, batch, seq, q_col, k_col, v_col):
    tile = ATTN_TILE
    n_q = seq // tile
    mla = kind == "mla"
    n_sub = 1 if mla else 2
    groups = MLA_HEADS if mla else FOX_HEADS // 2
    dk = QK_DEPTH if mla else LANES
    in_specs = [
        pl.BlockSpec((1, seq, dk), lambda b, g, *_: (b, 0, q_col + g)),
        pl.BlockSpec((1, seq, dk), lambda b, g, *_: (b, 0, k_col + g)),
        pl.BlockSpec((1, seq, LANES), lambda b, g, *_: (b, 0, v_col + g)),
    ]
    sweep_scr = _SweepScratch.shapes(n_sub, tile, n_q)
    operand_scr = [pltpu.VMEM((seq, QK_DEPTH), BF16), pltpu.VMEM((n_sub, seq, QK_DEPTH), BF16)]
    if kind == "fox":
        body = functools.partial(_fox_kernel, tile=tile, n_q=n_q)
        in_specs.append(pl.BlockSpec((1, seq, LANES), lambda b, g, *_: (b, 0, 0)))
        scratch = operand_scr + sweep_scr
    elif mla:
        body = functools.partial(_mla_kernel, tile=tile, n_q=n_q)
        scratch = sweep_scr
    else:
        body = functools.partial(_moba_kernel, tile=tile, n_q=n_q, n_blocks=seq // MOBA_BLOCK)
        in_specs.append(pl.BlockSpec((seq, LANES), lambda b, g, *_: (0, 0)))
        in_specs.append(pl.BlockSpec((N_TERMS * MOBA_HEADS, LANES), lambda b, g, *_: (0, 0)))
        scratch = operand_scr + sweep_scr
    return pl.pallas_call(
        body,
        grid_spec=pltpu.PrefetchScalarGridSpec(
            num_scalar_prefetch=2,
            grid=(batch, groups),
            in_specs=in_specs,
            out_specs=pl.BlockSpec((1, seq, LANES), lambda b, g, *_: (b, 0, g)),
            scratch_shapes=scratch),
        out_shape=jax.ShapeDtypeStruct((batch, seq, groups * LANES), BF16),
        compiler_params=pltpu.CompilerParams(
            dimension_semantics=("arbitrary", "arbitrary"),
            vmem_limit_bytes=VMEM_LIMIT_BYTES),
        name=kind + "_attn",
    )(*_sweep_items(n_q), q_arr, k_arr, v_arr, *extras)


def _combine_kernel(ya_ref, yb_ref, yc_ref, za_ref, zb_ref, zc_ref, x_ref, og_ref, w_ref, fg_ref,
                    o_ref, *, final):
    acc = x_ref[...]
    groups = ((ya_ref, za_ref), (yb_ref, zb_ref), (yc_ref, zc_ref))
    for idx, (y_ref, z_ref) in enumerate(groups):
        lo, hi = idx * FOX_WIDTH, (idx + 1) * FOX_WIDTH
        z = z_ref[...].astype(F32)
        gated = _rms(y_ref[...].astype(F32), og_ref[:, lo:hi]) * (z / (1.0 + jnp.exp(-z)))
        acc = acc + jnp.dot(gated.astype(BF16), w_ref[lo:hi, :], preferred_element_type=F32)
    if final:
        acc = _rms(acc, fg_ref[...])
    o_ref[...] = acc


def _combine(ya, yb, yc, proj, x2d, out_g, w_out, final_g, *, final):
    tokens = x2d.shape[0]
    tm = 512
    width = FOX_WIDTH
    zcol = lambda c: c * LANES // width
    y_spec = pl.BlockSpec((tm, width), lambda i: (i, 0))
    return pl.pallas_call(
        functools.partial(_combine_kernel, final=final),
        grid=(tokens // tm,),
        in_specs=[
            y_spec, y_spec, y_spec,
            pl.BlockSpec((tm, width), lambda i: (i, zcol(COL_FZ))),
            pl.BlockSpec((tm, width), lambda i: (i, zcol(COL_MZ))),
            pl.BlockSpec((tm, width), lambda i: (i, zcol(COL_BZ))),
            pl.BlockSpec((tm, D_MODEL), lambda i: (i, 0)),
            pl.BlockSpec((1, D_MIX), lambda i: (0, 0)),
            pl.BlockSpec((D_MIX, D_MODEL), lambda i: (0, 0)),
            pl.BlockSpec((1, D_MODEL), lambda i: (0, 0)),
        ],
        out_specs=pl.BlockSpec((tm, D_MODEL), lambda i: (i, 0)),
        out_shape=jax.ShapeDtypeStruct((tokens, D_MODEL), F32),
        compiler_params=pltpu.CompilerParams(
            dimension_semantics=("arbitrary",),
            vmem_limit_bytes=VMEM_LIMIT_BYTES),
        name="combine",
    )(ya, yb, yc, proj, proj, proj, x2d, out_g, w_out, final_g)


def _rot_cols(w):
    half = MLA_ROPE // 2
    return jnp.concatenate([-w[..., half:], w[..., :half]], axis=-1)


def _pack_layer(w_in, w_uq, w_ukv):
    cuts = np.cumsum(IN_SIZES)[:-1].tolist()
    fq, fk, fv, ff, fz, cq, ckv, kr, mz, bq, bk, bv, bz = jnp.split(w_in, cuts, axis=1)
    w_packed = jnp.concatenate(
        [fq, fk, fv, fz, bq, bk, bv, bz, mz, cq, ckv, kr, _rot_cols(kr)], axis=1).astype(BF16)
    wff_t = jnp.zeros((BF16_SUBLANES, D_MODEL), F32).at[:FOX_HEADS].set(ff.T).astype(BF16)
    wq = w_uq.reshape(MLA_Q_RANK, MLA_HEADS, MLA_QK)
    nope, rope = wq[..., :MLA_NOPE], wq[..., MLA_NOPE:]
    wq_packed = jnp.concatenate([nope, rope, _rot_cols(rope)], axis=-1)
    wq_packed = wq_packed.reshape(MLA_Q_RANK, MLA_HEADS * QK_DEPTH).astype(BF16)
    wkv = w_ukv.reshape(MLA_KV_RANK, MLA_HEADS, MLA_NOPE + MLA_V)
    wkv_packed = jnp.concatenate(
        [wkv[..., :MLA_NOPE].reshape(MLA_KV_RANK, MLA_WIDTH),
         wkv[..., MLA_NOPE:].reshape(MLA_KV_RANK, MLA_WIDTH)], axis=1).astype(BF16)
    return w_packed, wff_t, wq_packed, wkv_packed


def _moba_key_aux(seq):
    pos = np.arange(seq)
    aux = np.zeros((seq, LANES), np.float32)
    n_blocks = seq // MOBA_BLOCK
    aux[pos, pos // MOBA_BLOCK] = 1.0
    for t in range(N_TERMS):
        aux[:, n_blocks + 2 * t] = pos // MOBA_BLOCK
        aux[:, n_blocks + 2 * t + 1] = pos % MOBA_BLOCK
    return jnp.asarray(aux, BF16)


def kernel(x, ln_g, w_in, fox_b_f, mla_q_g, mla_w_uq, mla_kv_g, mla_w_ukv, out_g, w_out, final_g):
    batch, seq, _ = x.shape
    depth = ln_g.shape[0]
    tokens = batch * seq

    inv = ROPE_THETA ** (-jnp.arange(0, MLA_ROPE, 2, dtype=F32) / MLA_ROPE)
    ang = jnp.arange(seq, dtype=F32)[:, None] * inv[None, :]
    ang = jnp.concatenate([ang, ang], axis=-1)
    cs = jnp.concatenate([jnp.cos(ang), jnp.sin(ang)], axis=-1)
    slopes = 2.0 ** (-8.0 * jnp.arange(1, MOBA_HEADS + 1, dtype=F32) / MOBA_HEADS)
    slope_terms = jnp.concatenate([t.astype(F32) for t in _bf16_terms(slopes * LOG2E)])
    slope_terms = jnp.broadcast_to(slope_terms[:, None], (N_TERMS * MOBA_HEADS, LANES))
    moba_kaux = _moba_key_aux(seq)

    x2d = x.reshape(tokens, D_MODEL)
    for l in range(depth):
        w_packed, wff_t, wq_packed, wkv_packed = _pack_layer(w_in[l], mla_w_uq[l], mla_w_ukv[l])
        bf_col = jnp.zeros((BF16_SUBLANES, 1), F32).at[:FOX_HEADS, 0].set(fox_b_f[l])
        proj, caux = _inproj(x2d, ln_g[l][None, :], w_packed, wff_t, bf_col, seq=seq)
        q_cat, k_cat, v_mla = _mla_up(proj, cs, mla_q_g[l][None, :], mla_kv_g[l][None, :],
                                      wq_packed, wkv_packed, seq=seq)
        proj3 = proj.reshape(batch, seq, PACKED_COLS)
        ya = _attention("fox", proj3, proj3, proj3, [caux.reshape(batch, seq, LANES)],
                        batch=batch, seq=seq, q_col=COL_FQ, k_col=COL_FK, v_col=COL_FV)
        yb = _attention("mla", q_cat.reshape(batch, seq, -1), k_cat.reshape(batch, seq, -1),
                        v_mla.reshape(batch, seq, -1), [], batch=batch, seq=seq,
                        q_col=0, k_col=0, v_col=0)
        yc = _attention("moba", proj3, proj3, proj3, [moba_kaux, slope_terms],
                        batch=batch, seq=seq, q_col=COL_BQ, k_col=COL_BK, v_col=COL_BV)
        x2d = _combine(ya.reshape(tokens, -1), yb.reshape(tokens, -1), yc.reshape(tokens, -1),
                       proj, x2d, out_g[l][None, :], w_out[l].astype(BF16), final_g[None, :],
                       final=(l == depth - 1))
    return x2d.reshape(batch, seq, D_MODEL)
```

```python
import functools
from typing import Any, NamedTuple

import jax
import jax.numpy as jnp
import numpy as np
from jax import lax
from jax.experimental import pallas as pl
from jax.experimental.pallas import tpu as pltpu

D_MODEL = 1024
HEAD_DIM = 64
FOX_HEADS = 8
FOX_WIDTH = FOX_HEADS * HEAD_DIM
MLA_HEADS = 4
MLA_NOPE = 128
MLA_ROPE = 64
MLA_QK = MLA_NOPE + MLA_ROPE
MLA_V = 128
MLA_Q_RANK = 256
MLA_KV_RANK = 128
MLA_WIDTH = MLA_HEADS * MLA_V
MOBA_HEADS = 8
MOBA_WIDTH = MOBA_HEADS * HEAD_DIM
MOBA_BLOCK = 256
MOBA_TOPK = 3
D_MIX = FOX_WIDTH + MLA_WIDTH + MOBA_WIDTH
ROPE_THETA = 10000.0
EPS = 1e-6
IN_SIZES = (FOX_WIDTH, FOX_WIDTH, FOX_WIDTH, FOX_HEADS, FOX_WIDTH,
            MLA_Q_RANK, MLA_KV_RANK, MLA_ROPE, MLA_WIDTH,
            MOBA_WIDTH, MOBA_WIDTH, MOBA_WIDTH, MOBA_WIDTH)

LANES = 128
SUBLANES = 8
BF16_SUBLANES = 16
MXU_DEPTH = 256
VMEM_LIMIT_BYTES = 56 * 1024 * 1024

COL_FQ, COL_FK, COL_FV, COL_FZ = 0, 4, 8, 12
COL_BQ, COL_BK, COL_BV, COL_BZ = 16, 20, 24, 28
COL_MZ, COL_CQ, COL_CKV, COL_KR = 32, 36, 38, 39
PACKED_COLS = 40 * LANES
GATE_COLS = frozenset(c + k for c in (COL_FZ, COL_BZ, COL_MZ) for k in range(FOX_WIDTH // LANES))

QK_DEPTH = MXU_DEPTH
N_TERMS = 3
NEG = -1e30
LOG2E = 1.4426950408889634
ATTN_TILE = 512
SWEEP_UNROLL = 4

F32 = jnp.float32
BF16 = jnp.bfloat16


def _rms(x, g):
    return x * lax.rsqrt(jnp.mean(x * x, axis=-1, keepdims=True) + EPS) * g


def _nt_dot(a, b):
    return lax.dot_general(a, b, (((1,), (1,)), ((), ())), preferred_element_type=F32)


def _bf16_terms(x):
    hi = x.astype(BF16)
    mid = (x - hi.astype(F32)).astype(BF16)
    lo = (x - hi.astype(F32) - mid.astype(F32)).astype(BF16)
    return hi, mid, lo


def _transpose_to_rows(cols_t):
    n = cols_t.shape[1]
    blk = MXU_DEPTH
    eye = (lax.broadcasted_iota(jnp.int32, (blk, blk), 0)
           == lax.broadcasted_iota(jnp.int32, (blk, blk), 1)).astype(BF16)
    return jnp.concatenate(
        [_nt_dot(eye, cols_t[:, s:s + blk]) for s in range(0, n, blk)], axis=0)


def _silu(z):
    return z / (1.0 + jnp.exp(-z))


def _inproj_kernel(x_ref, g_ref, w_ref, wff_ref, bf_ref, proj_ref, caux_ref, h_scr, carry_scr,
                   *, tiles_per_seq, tn):
    i = pl.program_id(0)

    @pl.when(i % tiles_per_seq == 0)
    def _():
        carry_scr[...] = jnp.zeros_like(carry_scr)

    h = _rms(x_ref[...], g_ref[...]).astype(BF16)
    h_scr[...] = h
    logit = _nt_dot(wff_ref[...], h) + bf_ref[...]

    for start in range(0, PACKED_COLS, tn):
        acc = jnp.dot(h_scr[...], w_ref[:, start:start + tn], preferred_element_type=F32)
        blocks = [acc[:, c0:c0 + LANES] for c0 in range(0, tn, LANES)]
        blocks = [_silu(blk) if (start + k * LANES) // LANES in GATE_COLS else blk
                  for k, blk in enumerate(blocks)]
        proj_ref[:, start:start + tn] = jnp.concatenate(blocks, axis=1).astype(proj_ref.dtype)

    log_f = jnp.minimum(logit, 0.0) - jnp.log1p(jnp.exp(-jnp.abs(logit)))
    tm = log_f.shape[1]
    lane = lax.broadcasted_iota(jnp.int32, log_f.shape, 1)
    c = log_f
    d = 1
    while d < tm:
        c = c + jnp.where(lane >= d, pltpu.roll(c, d, axis=1), 0.0)
        d *= 2

    c = c + carry_scr[...][:, :1]
    carry_scr[...] = jnp.broadcast_to(c[:, tm - 1:tm], carry_scr.shape)
    terms = [t.astype(F32) for t in _bf16_terms(c[:FOX_HEADS] * LOG2E)]
    pad = jnp.zeros((LANES - N_TERMS * FOX_HEADS, tm), F32)
    caux_t = jnp.concatenate(terms + [pad], axis=0).astype(BF16)
    caux_ref[...] = _transpose_to_rows(caux_t).astype(caux_ref.dtype)


def _inproj(x2d, ln_g, w_packed, wff_t, bf_col, *, seq):
    tokens = x2d.shape[0]
    tm, tn = 512, 1024
    tiles_per_seq = seq // tm
    return pl.pallas_call(
        functools.partial(_inproj_kernel, tiles_per_seq=tiles_per_seq, tn=tn),
        grid=(tokens // tm,),
        in_specs=[
            pl.BlockSpec((tm, D_MODEL), lambda i: (i, 0)),
            pl.BlockSpec((1, D_MODEL), lambda i: (0, 0)),
            pl.BlockSpec((D_MODEL, PACKED_COLS), lambda i: (0, 0)),
            pl.BlockSpec((BF16_SUBLANES, D_MODEL), lambda i: (0, 0)),
            pl.BlockSpec((BF16_SUBLANES, 1), lambda i: (0, 0)),
        ],
        out_specs=[
            pl.BlockSpec((tm, PACKED_COLS), lambda i: (i, 0)),
            pl.BlockSpec((tm, LANES), lambda i: (i, 0)),
        ],
        out_shape=[
            jax.ShapeDtypeStruct((tokens, PACKED_COLS), BF16),
            jax.ShapeDtypeStruct((tokens, LANES), BF16),
        ],
        scratch_shapes=[
            pltpu.VMEM((tm, D_MODEL), BF16),
            pltpu.VMEM((BF16_SUBLANES, LANES), F32),
        ],
        compiler_params=pltpu.CompilerParams(
            dimension_semantics=("arbitrary",),
            vmem_limit_bytes=VMEM_LIMIT_BYTES),
        name="inproj",
    )(x2d, ln_g, w_packed, wff_t, bf_col)


def _mla_up_kernel(cq_ref, ckv_ref, krp_ref, cs_ref, gq_ref, gkv_ref, wq_ref, wkv_ref,
                   q_ref, k_ref, v_ref):
    lane = lax.broadcasted_iota(jnp.int32, (1, LANES), 1)
    cs = cs_ref[...]
    scale = MLA_QK ** -0.5 * LOG2E

    def rope_pair(a):
        t = a * cs
        return t + pltpu.roll(t, LANES // 2, axis=1)

    nq = _rms(cq_ref[...].astype(F32), gq_ref[...]).astype(BF16)
    qa = jnp.dot(nq, wq_ref[...], preferred_element_type=F32)
    for h in range(MLA_HEADS):
        base = h * QK_DEPTH
        q_ref[:, base:base + LANES] = (qa[:, base:base + LANES] * scale).astype(q_ref.dtype)
        rp = rope_pair(qa[:, base + LANES:base + QK_DEPTH])
        rp = jnp.where(lane < MLA_ROPE, rp, 0.0)
        q_ref[:, base + LANES:base + QK_DEPTH] = (rp * scale).astype(q_ref.dtype)

    nkv = _rms(ckv_ref[...].astype(F32), gkv_ref[...]).astype(BF16)
    kv = jnp.dot(nkv, wkv_ref[...], preferred_element_type=F32)
    kr = rope_pair(krp_ref[...].astype(F32)).astype(k_ref.dtype)
    for h in range(MLA_HEADS):
        base = h * QK_DEPTH
        k_ref[:, base:base + LANES] = kv[:, h * LANES:(h + 1) * LANES].astype(k_ref.dtype)
        k_ref[:, base + LANES:base + QK_DEPTH] = kr
    v_ref[...] = kv[:, MLA_HEADS * LANES:].astype(v_ref.dtype)


def _mla_up(proj, cs, g_q, g_kv, wq_packed, wkv_packed, *, seq):
    tokens = proj.shape[0]
    tm = 1024
    tiles_per_seq = seq // tm
    qk_cols = MLA_HEADS * QK_DEPTH
    return pl.pallas_call(
        _mla_up_kernel,
        grid=(tokens // tm,),
        in_specs=[
            pl.BlockSpec((tm, MLA_Q_RANK), lambda i: (i, COL_CQ * LANES // MLA_Q_RANK)),
            pl.BlockSpec((tm, LANES), lambda i: (i, COL_CKV)),
            pl.BlockSpec((tm, LANES), lambda i: (i, COL_KR)),
            pl.BlockSpec((tm, LANES), lambda i: (i % tiles_per_seq, 0)),
            pl.BlockSpec((1, MLA_Q_RANK), lambda i: (0, 0)),
            pl.BlockSpec((1, MLA_KV_RANK), lambda i: (0, 0)),
            pl.BlockSpec((MLA_Q_RANK, qk_cols), lambda i: (0, 0)),
            pl.BlockSpec((MLA_KV_RANK, 2 * MLA_WIDTH), lambda i: (0, 0)),
        ],
        out_specs=[
            pl.BlockSpec((tm, qk_cols), lambda i: (i, 0)),
            pl.BlockSpec((tm, qk_cols), lambda i: (i, 0)),
            pl.BlockSpec((tm, MLA_WIDTH), lambda i: (i, 0)),
        ],
        out_shape=[
            jax.ShapeDtypeStruct((tokens, qk_cols), BF16),
            jax.ShapeDtypeStruct((tokens, qk_cols), BF16),
            jax.ShapeDtypeStruct((tokens, MLA_WIDTH), BF16),
        ],
        compiler_params=pltpu.CompilerParams(
            dimension_semantics=("arbitrary",),
            vmem_limit_bytes=VMEM_LIMIT_BYTES),
        name="mla_up",
    )(proj, proj, proj, cs, g_q, g_kv, wq_packed, wkv_packed)


class _SweepScratch(NamedTuple):
    s: Any
    mx: Any
    p: Any
    alpha: Any
    m: Any
    l: Any
    acc: Any

    @classmethod
    def of(cls, refs):
        s0, s1, mx0, mx1, p0, p1, a0, a1, m, l, acc = refs
        return cls((s0, s1), (mx0, mx1), (p0, p1), (a0, a1), m, l, acc)

    @staticmethod
    def shapes(n_sub, tile, n_q):
        big = lambda dt: pltpu.VMEM((n_sub, tile, tile), dt)
        small = pltpu.VMEM((n_sub, tile, LANES), F32)
        state = pltpu.VMEM((n_q + 1, n_sub, tile, LANES), F32)
        return [big(F32)] * 2 + [small] * 2 + [big(BF16)] * 2 + [small] * 2 + [state] * 3


def _sweep_items(n_q):
    diag = [(i, i) for i in range(n_q)]
    past = [(i, t) for i in range(n_q) for t in range(i)]
    items = diag + [(n_q, 0)] + past + [(n_q, 0)]
    return (jnp.asarray([i for i, _ in items], jnp.int32),
            jnp.asarray([t for _, t in items], jnp.int32))


def _chunk_max(s):
    return functools.reduce(
        jnp.maximum, [s[:, c:c + LANES] for c in range(0, s.shape[1], LANES)])


def _flash_sweep(tabs, q_rows, k_tile, v_tile, scr, *, n_sub, tile, n_q):
    qi_tab, t_tab = tabs
    row = lax.broadcasted_iota(jnp.int32, (tile, tile), 0)
    col = lax.broadcasted_iota(jnp.int32, (tile, tile), 1)
    causal = col <= row

    def logits(j, buf):
        q_start = pl.multiple_of(qi_tab[j] * tile, tile)
        k_start = pl.multiple_of(t_tab[j] * tile, tile)
        for sub in range(n_sub):
            s = _nt_dot(q_rows(sub, q_start), k_tile(sub, k_start))
            scr.s[buf][sub] = s
            scr.mx[buf][sub] = _chunk_max(s)

    def softmax(j, buf, diag):
        slot = qi_tab[j]
        for sub in range(n_sub):
            if diag:
                s = jnp.where(causal, scr.s[buf][sub], NEG)
                m_next = jnp.broadcast_to(jnp.max(_chunk_max(s), axis=1, keepdims=True),
                                          (tile, LANES))
            else:
                s = scr.s[buf][sub]
                m_prev = scr.m[slot, sub]
                m_next = jnp.maximum(m_prev, jnp.max(scr.mx[buf][sub], axis=1, keepdims=True))
                alpha = jnp.exp2(m_prev - m_next)
                scr.alpha[buf][sub] = alpha
            p = [jnp.exp2((s[:, c:c + LANES] - m_next).astype(BF16))
                 for c in range(0, tile, LANES)]
            p_sum = functools.reduce(lambda a, b: a + b, p).astype(F32)
            scr.l[slot, sub] = p_sum if diag else alpha * scr.l[slot, sub] + p_sum
            scr.m[slot, sub] = m_next
            scr.p[buf][sub] = jnp.concatenate(p, axis=1)

    def weighted_values(j, buf, diag):
        slot = qi_tab[j]
        k_start = pl.multiple_of(t_tab[j] * tile, tile)
        for sub in range(n_sub):
            pv = jnp.dot(scr.p[buf][sub], v_tile(sub, k_start), preferred_element_type=F32)
            scr.acc[slot, sub] = pv if diag else scr.alpha[buf][sub] * scr.acc[slot, sub] + pv

    def phase(base, n_items, diag):
        assert n_items % SWEEP_UNROLL == 0 and SWEEP_UNROLL % 2 == 0
        last = base + n_items - 1
        logits(base, 0)
        logits(base + 1, 1)
        softmax(base, 0, diag)

        def steps(i, carry):
            for u in range(SWEEP_UNROLL):
                j = base + 1 + SWEEP_UNROLL * i + u
                buf = (1 + u) % 2
                weighted_values(j - 1, 1 - buf, diag)
                logits(jnp.minimum(j + 1, last), 1 - buf)
                softmax(j, buf, diag)
            return carry

        lax.fori_loop(0, n_items // SWEEP_UNROLL, steps, 0)

    phase(0, n_q, True)
    phase(n_q + 1, n_q * (n_q - 1) // 2, False)


def _flash_result(scr, qi, sub):
    return scr.acc[qi, sub] / jnp.sum(scr.l[qi, sub], axis=1, keepdims=True)


def _head_lane_mask(sub):
    lane = lax.broadcasted_iota(jnp.int32, (1, LANES), 1)
    return (lane < HEAD_DIM) if sub == 0 else (lane >= HEAD_DIM)


def _store_head_pair(o_ref, scr, *, tile, n_q):
    for qi in range(n_q):
        merged = jnp.where(_head_lane_mask(0), _flash_result(scr, qi, 0), _flash_result(scr, qi, 1))
        o_ref[0, qi * tile:(qi + 1) * tile, :] = merged.astype(o_ref.dtype)


def _fox_kernel(qi_tab, t_tab, q_ref, k_ref, v_ref, caux_ref, o_ref, kx_scr, qx_scr, *sweep_refs,
                tile, n_q):
    g = pl.program_id(1)
    scr = _SweepScratch.of(sweep_refs)
    kx_scr[:, :LANES] = k_ref[0]
    kx_scr[:, LANES:] = caux_ref[0]
    q = (q_ref[0].astype(F32) * (HEAD_DIM ** -0.5 * LOG2E)).astype(q_ref.dtype)
    lane = lax.broadcasted_iota(jnp.int32, q.shape, 1)
    for sub in range(2):
        head = 2 * g + sub
        qx_scr[sub, :, :LANES] = jnp.where(_head_lane_mask(sub), q, jnp.zeros_like(q))
        picks = functools.reduce(
            lambda a, b: a | b, [lane == head + t * FOX_HEADS for t in range(N_TERMS)])
        qx_scr[sub, :, LANES:] = jnp.where(picks, -1.0, 0.0).astype(qx_scr.dtype)

    _flash_sweep((qi_tab, t_tab),
                 lambda sub, start: qx_scr[sub, pl.ds(start, tile), :],
                 lambda sub, start: kx_scr[pl.ds(start, tile), :],
                 lambda sub, start: v_ref[0, pl.ds(start, tile), :],
                 scr, n_sub=2, tile=tile, n_q=n_q)
    _store_head_pair(o_ref, scr, tile=tile, n_q=n_q)


def _mla_kernel(qi_tab, t_tab, q_ref, k_ref, v_ref, o_ref, *sweep_refs, tile, n_q):
    scr = _SweepScratch.of(sweep_refs)
    _flash_sweep((qi_tab, t_tab),
                 lambda sub, start: q_ref[0, pl.ds(start, tile), :],
                 lambda sub, start: k_ref[0, pl.ds(start, tile), :],
                 lambda sub, start: v_ref[0, pl.ds(start, tile), :],
                 scr, n_sub=1, tile=tile, n_q=n_q)
    for qi in range(n_q):
        o_ref[0, qi * tile:(qi + 1) * tile, :] = _flash_result(scr, qi, 0).astype(o_ref.dtype)


def _moba_kernel(qi_tab, t_tab, q_ref, k_ref, v_ref, kaux_ref, slope_ref, o_ref,
                 kx_scr, qx_scr, *sweep_refs, tile, n_q, n_blocks):
    blk = MOBA_BLOCK
    g = pl.program_id(1)
    scr = _SweepScratch.of(sweep_refs)
    seq = q_ref.shape[1]

    k = k_ref[0]
    kx_scr[:, :LANES] = k
    kx_scr[:, LANES:] = kaux_ref[...]
    kmean = jnp.sum(k.astype(F32).reshape(n_blocks, blk, LANES), axis=1) * (1.0 / blk)
    kmean_terms = _bf16_terms(kmean)

    q_raw = q_ref[0]
    q_scaled = (q_raw.astype(F32) * (HEAD_DIM ** -0.5 * LOG2E)).astype(q_raw.dtype)
    blk_id = lax.broadcasted_iota(jnp.int32, (n_blocks, blk), 0)
    row_id = lax.broadcasted_iota(jnp.int32, (LANES - n_blocks, seq), 0)
    for sub in range(2):
        head_mask = _head_lane_mask(sub)
        q_head = jnp.where(head_mask, q_raw, jnp.zeros_like(q_raw))
        gate = functools.reduce(
            lambda a, b: a + b, [_nt_dot(term, q_head) for term in kmean_terms])
        sel_cols = []
        for own in range(n_blocks):
            g_own = gate[:, own * blk:(own + 1) * blk]
            rank = jnp.zeros((n_blocks, blk), jnp.int32)
            for m in range(own):
                gm = g_own[m:m + 1, :]
                ahead = (gm > g_own) | ((gm == g_own) & (blk_id > m))
                rank = rank + ahead.astype(jnp.int32)
            allowed = ((rank < MOBA_TOPK) & (blk_id < own)) | (blk_id == own)
            sel_cols.append(jnp.where(allowed, 0.0, NEG))
        sel_t = jnp.concatenate(sel_cols, axis=1)
        alibi_t = jnp.zeros(row_id.shape, F32)
        for t in range(N_TERMS):
            term = slope_ref[pl.ds(t * MOBA_HEADS + 2 * g + sub, 1), :]
            term = jnp.concatenate([term] * (seq // LANES), axis=1)
            alibi_t = jnp.where(row_id == 2 * t, term * float(blk),
                                jnp.where(row_id == 2 * t + 1, term, alibi_t))
        aux_t = jnp.concatenate([sel_t, alibi_t], axis=0).astype(BF16)
        qx_scr[sub, :, :LANES] = jnp.where(head_mask, q_scaled, jnp.zeros_like(q_scaled))
        qx_scr[sub, :, LANES:] = _transpose_to_rows(aux_t).astype(qx_scr.dtype)

    _flash_sweep((qi_tab, t_tab),
                 lambda sub, start: qx_scr[sub, pl.ds(start, tile), :],
                 lambda sub, start: kx_scr[pl.ds(start, tile), :],
                 lambda sub, start: v_ref[0, pl.ds(start, tile), :],
                 scr, n_sub=2, tile=tile, n_q=n_q)
    _store_head_pair(o_ref, scr, tile=tile, n_q=n_q)


def _attention(kind, q_arr, k_arr, v_arr, extras, *, batch, seq, q_col, k_col, v_col):
    tile = ATTN_TILE
    n_q = seq // tile
    mla = kind == "mla"
    n_sub = 1 if mla else 2
    groups = MLA_HEADS if mla else FOX_HEADS // 2
    dk = QK_DEPTH if mla else LANES
    in_specs = [
        pl.BlockSpec((1, seq, dk), lambda b, g, *_: (b, 0, q_col + g)),
        pl.BlockSpec((1, seq, dk), lambda b, g, *_: (b, 0, k_col + g)),
        pl.BlockSpec((1, seq, LANES), lambda b, g, *_: (b, 0, v_col + g)),
    ]
    sweep_scr = _SweepScratch.shapes(n_sub, tile, n_q)
    operand_scr = [pltpu.VMEM((seq, QK_DEPTH), BF16), pltpu.VMEM((n_sub, seq, QK_DEPTH), BF16)]
    if kind == "fox":
        body = functools.partial(_fox_kernel, tile=tile, n_q=n_q)
        in_specs.append(pl.BlockSpec((1, seq, LANES), lambda b, g, *_: (b, 0, 0)))
        scratch = operand_scr + sweep_scr
    elif mla:
        body = functools.partial(_mla_kernel, tile=tile, n_q=n_q)
        scratch = sweep_scr
    else:
        body = functools.partial(_moba_kernel, tile=tile, n_q=n_q, n_blocks=seq // MOBA_BLOCK)
        in_specs.append(pl.BlockSpec((seq, LANES), lambda b, g, *_: (0, 0)))
        in_specs.append(pl.BlockSpec((N_TERMS * MOBA_HEADS, LANES), lambda b, g, *_: (0, 0)))
        scratch = operand_scr + sweep_scr
    return pl.pallas_call(
        body,
        grid_spec=pltpu.PrefetchScalarGridSpec(
            num_scalar_prefetch=2,
            grid=(batch, groups),
            in_specs=in_specs,
            out_specs=pl.BlockSpec((1, seq, LANES), lambda b, g, *_: (b, 0, g)),
            scratch_shapes=scratch),
        out_shape=jax.ShapeDtypeStruct((batch, seq, groups * LANES), BF16),
        compiler_params=pltpu.CompilerParams(
            dimension_semantics=("arbitrary", "arbitrary"),
            vmem_limit_bytes=VMEM_LIMIT_BYTES),
        name=kind + "_attn",
    )(*_sweep_items(n_q), q_arr, k_arr, v_arr, *extras)


def _combine_kernel(ya_ref, yb_ref, yc_ref, za_ref, zb_ref, zc_ref, x_ref, og_ref, w_ref, fg_ref,
                    o_ref, *, final):
    acc = x_ref[...]
    groups = ((ya_ref, za_ref), (yb_ref, zb_ref), (yc_ref, zc_ref))
    for idx, (y_ref, z_ref) in enumerate(groups):
        lo, hi = idx * FOX_WIDTH, (idx + 1) * FOX_WIDTH
        gated = _rms(y_ref[...].astype(F32), og_ref[:, lo:hi]) * z_ref[...].astype(F32)
        acc = acc + jnp.dot(gated.astype(BF16), w_ref[lo:hi, :], preferred_element_type=F32)
    if final:
        acc = _rms(acc, fg_ref[...])
    o_ref[...] = acc


def _combine(ya, yb, yc, proj, x2d, out_g, w_out, final_g, *, final):
    tokens = x2d.shape[0]
    tm = 512
    width = FOX_WIDTH
    zcol = lambda c: c * LANES // width
    y_spec = pl.BlockSpec((tm, width), lambda i: (i, 0))
    return pl.pallas_call(
        functools.partial(_combine_kernel, final=final),
        grid=(tokens // tm,),
        in_specs=[
            y_spec, y_spec, y_spec,
            pl.BlockSpec((tm, width), lambda i: (i, zcol(COL_FZ))),
            pl.BlockSpec((tm, width), lambda i: (i, zcol(COL_MZ))),
            pl.BlockSpec((tm, width), lambda i: (i, zcol(COL_BZ))),
            pl.BlockSpec((tm, D_MODEL), lambda i: (i, 0)),
            pl.BlockSpec((1, D_MIX), lambda i: (0, 0)),
            pl.BlockSpec((D_MIX, D_MODEL), lambda i: (0, 0)),
            pl.BlockSpec((1, D_MODEL), lambda i: (0, 0)),
        ],
        out_specs=pl.BlockSpec((tm, D_MODEL), lambda i: (i, 0)),
        out_shape=jax.ShapeDtypeStruct((tokens, D_MODEL), F32),
        compiler_params=pltpu.CompilerParams(
            dimension_semantics=("arbitrary",),
            vmem_limit_bytes=VMEM_LIMIT_BYTES),
        name="combine",
    )(ya, yb, yc, proj, proj, proj, x2d, out_g, w_out, final_g)


def _rot_cols(w):
    half = MLA_ROPE // 2
    return jnp.concatenate([-w[..., half:], w[..., :half]], axis=-1)


def _pack_layer(w_in, w_uq, w_ukv):
    cuts = np.cumsum(IN_SIZES)[:-1].tolist()
    fq, fk, fv, ff, fz, cq, ckv, kr, mz, bq, bk, bv, bz = jnp.split(w_in, cuts, axis=1)
    w_packed = jnp.concatenate(
        [fq, fk, fv, fz, bq, bk, bv, bz, mz, cq, ckv, kr, _rot_cols(kr)], axis=1).astype(BF16)
    wff_t = jnp.zeros((BF16_SUBLANES, D_MODEL), F32).at[:FOX_HEADS].set(ff.T).astype(BF16)
    wq = w_uq.reshape(MLA_Q_RANK, MLA_HEADS, MLA_QK)
    nope, rope = wq[..., :MLA_NOPE], wq[..., MLA_NOPE:]
    wq_packed = jnp.concatenate([nope, rope, _rot_cols(rope)], axis=-1)
    wq_packed = wq_packed.reshape(MLA_Q_RANK, MLA_HEADS * QK_DEPTH).astype(BF16)
    wkv = w_ukv.reshape(MLA_KV_RANK, MLA_HEADS, MLA_NOPE + MLA_V)
    wkv_packed = jnp.concatenate(
        [wkv[..., :MLA_NOPE].reshape(MLA_KV_RANK, MLA_WIDTH),
         wkv[..., MLA_NOPE:].reshape(MLA_KV_RANK, MLA_WIDTH)], axis=1).astype(BF16)
    return w_packed, wff_t, wq_packed, wkv_packed


def _moba_key_aux(seq):
    pos = np.arange(seq)
    aux = np.zeros((seq, LANES), np.float32)
    n_blocks = seq // MOBA_BLOCK
    aux[pos, pos // MOBA_BLOCK] = 1.0
    for t in range(N_TERMS):
        aux[:, n_blocks + 2 * t] = pos // MOBA_BLOCK
        aux[:, n_blocks + 2 * t + 1] = pos % MOBA_BLOCK
    return jnp.asarray(aux, BF16)


def kernel(x, ln_g, w_in, fox_b_f, mla_q_g, mla_w_uq, mla_kv_g, mla_w_ukv, out_g, w_out, final_g):
    batch, seq, _ = x.shape
    depth = ln_g.shape[0]
    tokens = batch * seq

    inv = ROPE_THETA ** (-jnp.arange(0, MLA_ROPE, 2, dtype=F32) / MLA_ROPE)
    ang = jnp.arange(seq, dtype=F32)[:, None] * inv[None, :]
    ang = jnp.concatenate([ang, ang], axis=-1)
    cs = jnp.concatenate([jnp.cos(ang), jnp.sin(ang)], axis=-1)
    slopes = 2.0 ** (-8.0 * jnp.arange(1, MOBA_HEADS + 1, dtype=F32) / MOBA_HEADS)
    slope_terms = jnp.concatenate([t.astype(F32) for t in _bf16_terms(slopes * LOG2E)])
    slope_terms = jnp.broadcast_to(slope_terms[:, None], (N_TERMS * MOBA_HEADS, LANES))
    moba_kaux = _moba_key_aux(seq)

    x2d = x.reshape(tokens, D_MODEL)
    for l in range(depth):
        w_packed, wff_t, wq_packed, wkv_packed = _pack_layer(w_in[l], mla_w_uq[l], mla_w_ukv[l])
        bf_col = jnp.zeros((BF16_SUBLANES, 1), F32).at[:FOX_HEADS, 0].set(fox_b_f[l])
        proj, caux = _inproj(x2d, ln_g[l][None, :], w_packed, wff_t, bf_col, seq=seq)
        q_cat, k_cat, v_mla = _mla_up(proj, cs, mla_q_g[l][None, :], mla_kv_g[l][None, :],
                                      wq_packed, wkv_packed, seq=seq)
        proj3 = proj.reshape(batch, seq, PACKED_COLS)
        ya = _attention("fox", proj3, proj3, proj3, [caux.reshape(batch, seq, LANES)],
                        batch=batch, seq=seq, q_col=COL_FQ, k_col=COL_FK, v_col=COL_FV)
        yb = _attention("mla", q_cat.reshape(batch, seq, -1), k_cat.reshape(batch, seq, -1),
                        v_mla.reshape(batch, seq, -1), [], batch=batch, seq=seq,
                        q_col=0, k_col=0, v_col=0)
        yc = _attention("moba", proj3, proj3, proj3, [moba_kaux, slope_terms],
                        batch=batch, seq=seq, q_col=COL_BQ, k_col=COL_BK, v_col=COL_BV)
        x2d = _combine(ya.reshape(tokens, -1), yb.reshape(tokens, -1), yc.reshape(tokens, -1),
                       proj, x2d, out_g[l][None, :], w_out[l].astype(BF16), final_g[None, :],
                       final=(l == depth - 1))
    return x2d.reshape(batch, seq, D_MODEL)
```

```python
import functools
from typing import Any, NamedTuple

import jax
import jax.numpy as jnp
import numpy as np
from jax import lax
from jax.experimental import pallas as pl
from jax.experimental.pallas import tpu as pltpu

D_MODEL = 1024
HEAD_DIM = 64
FOX_HEADS = 8
FOX_WIDTH = FOX_HEADS * HEAD_DIM
MLA_HEADS = 4
MLA_NOPE = 128
MLA_ROPE = 64
MLA_QK = MLA_NOPE + MLA_ROPE
MLA_V = 128
MLA_Q_RANK = 256
MLA_KV_RANK = 128
MLA_WIDTH = MLA_HEADS * MLA_V
MOBA_HEADS = 8
MOBA_WIDTH = MOBA_HEADS * HEAD_DIM
MOBA_BLOCK = 256
MOBA_TOPK = 3
D_MIX = FOX_WIDTH + MLA_WIDTH + MOBA_WIDTH
ROPE_THETA = 10000.0
EPS = 1e-6
IN_SIZES = (FOX_WIDTH, FOX_WIDTH, FOX_WIDTH, FOX_HEADS, FOX_WIDTH,
            MLA_Q_RANK, MLA_KV_RANK, MLA_ROPE, MLA_WIDTH,
            MOBA_WIDTH, MOBA_WIDTH, MOBA_WIDTH, MOBA_WIDTH)

LANES = 128
SUBLANES = 8
BF16_SUBLANES = 16
MXU_DEPTH = 256
VMEM_LIMIT_BYTES = 56 * 1024 * 1024

COL_FQ, COL_FK, COL_FV, COL_FZ = 0, 4, 8, 12
COL_BQ, COL_BK, COL_BV, COL_BZ = 16, 20, 24, 28
COL_MZ, COL_CQ, COL_CKV, COL_KR = 32, 36, 38, 39
PACKED_COLS = 40 * LANES
GATE_COLS = frozenset(c + k for c in (COL_FZ, COL_BZ, COL_MZ) for k in range(FOX_WIDTH // LANES))

QK_DEPTH = MXU_DEPTH
N_TERMS = 3
NEG = -1e30
LOG2E = 1.4426950408889634
ATTN_TILE = 512
SWEEP_UNROLL = 4

F32 = jnp.float32
BF16 = jnp.bfloat16


def _rms(x, g):
    return x * lax.rsqrt(jnp.mean(x * x, axis=-1, keepdims=True) + EPS) * g


def _nt_dot(a, b):
    return lax.dot_general(a, b, (((1,), (1,)), ((), ())), preferred_element_type=F32)


def _bf16_terms(x):
    hi = x.astype(BF16)
    mid = (x - hi.astype(F32)).astype(BF16)
    lo = (x - hi.astype(F32) - mid.astype(F32)).astype(BF16)
    return hi, mid, lo


def _transpose_to_rows(cols_t):
    n = cols_t.shape[1]
    blk = MXU_DEPTH
    eye = (lax.broadcasted_iota(jnp.int32, (blk, blk), 0)
           == lax.broadcasted_iota(jnp.int32, (blk, blk), 1)).astype(BF16)
    return jnp.concatenate(
        [_nt_dot(eye, cols_t[:, s:s + blk]) for s in range(0, n, blk)], axis=0)


def _silu(z):
    return z / (1.0 + jnp.exp(-z))


def _inproj_kernel(x_ref, g_ref, w_ref, wff_ref, bf_ref, proj_ref, caux_ref, h_scr, carry_scr,
                   *, tiles_per_seq, tn):
    i = pl.program_id(0)

    @pl.when(i % tiles_per_seq == 0)
    def _():
        carry_scr[...] = jnp.zeros_like(carry_scr)

    h = _rms(x_ref[...], g_ref[...]).astype(BF16)
    h_scr[...] = h
    logit = _nt_dot(wff_ref[...], h) + bf_ref[...]

    for start in range(0, PACKED_COLS, tn):
        acc = jnp.dot(h_scr[...], w_ref[:, start:start + tn], preferred_element_type=F32)
        blocks = [acc[:, c0:c0 + LANES] for c0 in range(0, tn, LANES)]
        blocks = [_silu(blk) if (start + k * LANES) // LANES in GATE_COLS else blk
                  for k, blk in enumerate(blocks)]
        proj_ref[:, start:start + tn] = jnp.concatenate(blocks, axis=1).astype(proj_ref.dtype)

    log_f = jnp.minimum(logit, 0.0) - jnp.log1p(jnp.exp(-jnp.abs(logit)))
    tm = log_f.shape[1]
    lane = lax.broadcasted_iota(jnp.int32, log_f.shape, 1)
    c = log_f
    d = 1
    while d < tm:
        c = c + jnp.where(lane >= d, pltpu.roll(c, d, axis=1), 0.0)
        d *= 2

    c = c + carry_scr[...][:, :1]
    carry_scr[...] = jnp.broadcast_to(c[:, tm - 1:tm], carry_scr.shape)
    terms = [t.astype(F32) for t in _bf16_terms(c[:FOX_HEADS] * LOG2E)]
    pad = jnp.zeros((LANES - N_TERMS * FOX_HEADS, tm), F32)
    caux_t = jnp.concatenate(terms + [pad], axis=0).astype(BF16)
    caux_ref[...] = _transpose_to_rows(caux_t).astype(caux_ref.dtype)


def _inproj(x2d, ln_g, w_packed, wff_t, bf_col, *, seq):
    tokens = x2d.shape[0]
    tm, tn = 512, 1024
    tiles_per_seq = seq // tm
    return pl.pallas_call(
        functools.partial(_inproj_kernel, tiles_per_seq=tiles_per_seq, tn=tn),
        grid=(tokens // tm,),
        in_specs=[
            pl.BlockSpec((tm, D_MODEL), lambda i: (i, 0)),
            pl.BlockSpec((1, D_MODEL), lambda i: (0, 0)),
            pl.BlockSpec((D_MODEL, PACKED_COLS), lambda i: (0, 0)),
            pl.BlockSpec((BF16_SUBLANES, D_MODEL), lambda i: (0, 0)),
            pl.BlockSpec((BF16_SUBLANES, 1), lambda i: (0, 0)),
        ],
        out_specs=[
            pl.BlockSpec((tm, PACKED_COLS), lambda i: (i, 0)),
            pl.BlockSpec((tm, LANES), lambda i: (i, 0)),
        ],
        out_shape=[
            jax.ShapeDtypeStruct((tokens, PACKED_COLS), BF16),
            jax.ShapeDtypeStruct((tokens, LANES), BF16),
        ],
        scratch_shapes=[
            pltpu.VMEM((tm, D_MODEL), BF16),
            pltpu.VMEM((BF16_SUBLANES, LANES), F32),
        ],
        compiler_params=pltpu.CompilerParams(
            dimension_semantics=("arbitrary",),
            vmem_limit_bytes=VMEM_LIMIT_BYTES),
        name="inproj",
    )(x2d, ln_g, w_packed, wff_t, bf_col)


def _mla_up_kernel(cq_ref, ckv_ref, krp_ref, cs_ref, gq_ref, gkv_ref, wq_ref, wkv_ref,
                   q_ref, k_ref, v_ref):
    lane = lax.broadcasted_iota(jnp.int32, (1, LANES), 1)
    cs = cs_ref[...]
    scale = MLA_QK ** -0.5 * LOG2E

    def rope_pair(a):
        t = a * cs
        return t + pltpu.roll(t, LANES // 2, axis=1)

    nq = _rms(cq_ref[...].astype(F32), gq_ref[...]).astype(BF16)
    qa = jnp.dot(nq, wq_ref[...], preferred_element_type=F32)
    for h in range(MLA_HEADS):
        base = h * QK_DEPTH
        q_ref[:, base:base + LANES] = (qa[:, base:base + LANES] * scale).astype(q_ref.dtype)
        rp = rope_pair(qa[:, base + LANES:base + QK_DEPTH])
        rp = jnp.where(lane < MLA_ROPE, rp, 0.0)
        q_ref[:, base + LANES:base + QK_DEPTH] = (rp * scale).astype(q_ref.dtype)

    nkv = _rms(ckv_ref[...].astype(F32), gkv_ref[...]).astype(BF16)
    kv = jnp.dot(nkv, wkv_ref[...], preferred_element_type=F32)
    kr = rope_pair(krp_ref[...].astype(F32)).astype(k_ref.dtype)
    for h in range(MLA_HEADS):
        base = h * QK_DEPTH
        k_ref[:, base:base + LANES] = kv[:, h * LANES:(h + 1) * LANES].astype(k_ref.dtype)
        k_ref[:, base + LANES:base + QK_DEPTH] = kr
    v_ref[...] = kv[:, MLA_HEADS * LANES:].astype(v_ref.dtype)


def _mla_up(proj, cs, g_q, g_kv, wq_packed, wkv_packed, *, seq):
    tokens = proj.shape[0]
    tm = 1024
    tiles_per_seq = seq // tm
    qk_cols = MLA_HEADS * QK_DEPTH
    return pl.pallas_call(
        _mla_up_kernel,
        grid=(tokens // tm,),
        in_specs=[
            pl.BlockSpec((tm, MLA_Q_RANK), lambda i: (i, COL_CQ * LANES // MLA_Q_RANK)),
            pl.BlockSpec((tm, LANES), lambda i: (i, COL_CKV)),
            pl.BlockSpec((tm, LANES), lambda i: (i, COL_KR)),
            pl.BlockSpec((tm, LANES), lambda i: (i % tiles_per_seq, 0)),
            pl.BlockSpec((1, MLA_Q_RANK), lambda i: (0, 0)),
            pl.BlockSpec((1, MLA_KV_RANK), lambda i: (0, 0)),
            pl.BlockSpec((MLA_Q_RANK, qk_cols), lambda i: (0, 0)),
            pl.BlockSpec((MLA_KV_RANK, 2 * MLA_WIDTH), lambda i: (0, 0)),
        ],
        out_specs=[
            pl.BlockSpec((tm, qk_cols), lambda i: (i, 0)),
            pl.BlockSpec((tm, qk_cols), lambda i: (i, 0)),
            pl.BlockSpec((tm, MLA_WIDTH), lambda i: (i, 0)),
        ],
        out_shape=[
            jax.ShapeDtypeStruct((tokens, qk_cols), BF16),
            jax.ShapeDtypeStruct((tokens, qk_cols), BF16),
            jax.ShapeDtypeStruct((tokens, MLA_WIDTH), BF16),
        ],
        compiler_params=pltpu.CompilerParams(
            dimension_semantics=("arbitrary",),
            vmem_limit_bytes=VMEM_LIMIT_BYTES),
        name="mla_up",
    )(proj, proj, proj, cs, g_q, g_kv, wq_packed, wkv_packed)


class _SweepScratch(NamedTuple):
    s: Any
    mx: Any
    p: Any
    alpha: Any
    m: Any
    l: Any
    acc: Any

    @classmethod
    def of(cls, refs):
        s0, s1, mx0, mx1, p0, p1, a0, a1, m, l, acc = refs
        return cls((s0, s1), (mx0, mx1), (p0, p1), (a0, a1), m, l, acc)

    @staticmethod
    def shapes(n_sub, tile, n_q):
        big = lambda dt: pltpu.VMEM((n_sub, tile, tile), dt)
        small = pltpu.VMEM((n_sub, tile, LANES), F32)
        state = pltpu.VMEM((n_q, n_sub, tile, LANES), F32)
        return [big(F32)] * 2 + [small] * 2 + [big(BF16)] * 2 + [small] * 2 + [state] * 3


def _sweep_items(n_q):
    items = [(i, i) for i in range(n_q)] + [(i, t) for i in range(n_q) for t in range(i)]
    return (jnp.asarray([i for i, _ in items], jnp.int32),
            jnp.asarray([t for _, t in items], jnp.int32))


def _chunk_max(s):
    return functools.reduce(
        jnp.maximum, [s[:, c:c + LANES] for c in range(0, s.shape[1], LANES)])


def _flash_sweep(tabs, q_rows, k_tile, v_tile, scr, *, n_sub, tile, n_q):
    qi_tab, t_tab = tabs
    row = lax.broadcasted_iota(jnp.int32, (tile, tile), 0)
    col = lax.broadcasted_iota(jnp.int32, (tile, tile), 1)
    causal = col <= row

    def logits(j, buf):
        q_start = pl.multiple_of(qi_tab[j] * tile, tile)
        k_start = pl.multiple_of(t_tab[j] * tile, tile)
        for sub in range(n_sub):
            s = _nt_dot(q_rows(sub, q_start), k_tile(sub, k_start))
            scr.s[buf][sub] = s
            scr.mx[buf][sub] = _chunk_max(s)

    def softmax(j, buf, diag):
        slot = qi_tab[j]
        for sub in range(n_sub):
            if diag:
                s = jnp.where(causal, scr.s[buf][sub], NEG)
                m_next = jnp.broadcast_to(jnp.max(_chunk_max(s), axis=1, keepdims=True),
                                          (tile, LANES))
            else:
                s = scr.s[buf][sub]
                m_prev = scr.m[slot, sub]
                m_next = jnp.maximum(m_prev, jnp.max(scr.mx[buf][sub], axis=1, keepdims=True))
                alpha = jnp.exp2(m_prev - m_next)
                scr.alpha[buf][sub] = alpha
            p = [jnp.exp2(s[:, c:c + LANES] - m_next) for c in range(0, tile, LANES)]
            p_sum = functools.reduce(lambda a, b: a + b, p)
            scr.l[slot, sub] = p_sum if diag else alpha * scr.l[slot, sub] + p_sum
            scr.m[slot, sub] = m_next
            scr.p[buf][sub] = jnp.concatenate(p, axis=1).astype(BF16)

    def weighted_values(j, buf, diag):
        slot = qi_tab[j]
        k_start = pl.multiple_of(t_tab[j] * tile, tile)
        for sub in range(n_sub):
            pv = jnp.dot(scr.p[buf][sub], v_tile(sub, k_start), preferred_element_type=F32)
            scr.acc[slot, sub] = pv if diag else scr.alpha[buf][sub] * scr.acc[slot, sub] + pv

    n_items = n_q * (n_q + 1) // 2
    tail = 4
    assert SWEEP_UNROLL % 2 == 0 and (n_items - n_q - tail) % SWEEP_UNROLL == 0

    def step(j, buf, *, ahead=True, attend=True):
        static = isinstance(j, int)
        weighted_values(j - 1, 1 - buf, static and j - 1 < n_q)
        if ahead:
            logits(j + 1, 1 - buf)
        if attend:
            softmax(j, buf, static and j < n_q)

    logits(0, 0)
    logits(1, 1)
    softmax(0, 0, True)
    for j in range(1, n_q + 1):
        step(j, j % 2)

    def steps(i, carry):
        for u in range(SWEEP_UNROLL):
            step(n_q + 1 + SWEEP_UNROLL * i + u, (n_q + 1 + u) % 2)
        return carry

    lax.fori_loop(0, (n_items - n_q - tail) // SWEEP_UNROLL, steps, 0)
    for j in range(n_items - tail + 1, n_items + 1):
        step(j, j % 2, ahead=j + 1 < n_items, attend=j < n_items)


def _flash_result(scr, qi, sub):
    return scr.acc[qi, sub] / jnp.sum(scr.l[qi, sub], axis=1, keepdims=True)


def _head_lane_mask(sub):
    lane = lax.broadcasted_iota(jnp.int32, (1, LANES), 1)
    return (lane < HEAD_DIM) if sub == 0 else (lane >= HEAD_DIM)


def _store_head_pair(o_ref, scr, *, tile, n_q):
    for qi in range(n_q):
        merged = jnp.where(_head_lane_mask(0), _flash_result(scr, qi, 0), _flash_result(scr, qi, 1))
        o_ref[0, qi * tile:(qi + 1) * tile, :] = merged.astype(o_ref.dtype)


def _fox_kernel(qi_tab, t_tab, q_ref, k_ref, v_ref, caux_ref, o_ref, kx_scr, qx_scr, *sweep_refs,
                tile, n_q):
    g = pl.program_id(1)
    scr = _SweepScratch.of(sweep_refs)
    kx_scr[:, :LANES] = k_ref[0]
    kx_scr[:, LANES:] = caux_ref[0]
    q = (q_ref[0].astype(F32) * (HEAD_DIM ** -0.5 * LOG2E)).astype(q_ref.dtype)
    lane = lax.broadcasted_iota(jnp.int32, q.shape, 1)
    for sub in range(2):
        head = 2 * g + sub
        qx_scr[sub, :, :LANES] = jnp.where(_head_lane_mask(sub), q, jnp.zeros_like(q))
        picks = functools.reduce(
            lambda a, b: a | b, [lane == head + t * FOX_HEADS for t in range(N_TERMS)])
        qx_scr[sub, :, LANES:] = jnp.where(picks, -1.0, 0.0).astype(qx_scr.dtype)

    _flash_sweep((qi_tab, t_tab),
                 lambda sub, start: qx_scr[sub, pl.ds(start, tile), :],
                 lambda sub, start: kx_scr[pl.ds(start, tile), :],
                 lambda sub, start: v_ref[0, pl.ds(start, tile), :],
                 scr, n_sub=2, tile=tile, n_q=n_q)
    _store_head_pair(o_ref, scr, tile=tile, n_q=n_q)


def _mla_kernel(qi_tab, t_tab, q_ref, k_ref, v_ref, o_ref, *sweep_refs, tile, n_q):
    scr = _SweepScratch.of(sweep_refs)
    _flash_sweep((qi_tab, t_tab),
                 lambda sub, start: q_ref[0, pl.ds(start, tile), :],
                 lambda sub, start: k_ref[0, pl.ds(start, tile), :],
                 lambda sub, start: v_ref[0, pl.ds(start, tile), :],
                 scr, n_sub=1, tile=tile, n_q=n_q)
    for qi in range(n_q):
        o_ref[0, qi * tile:(qi + 1) * tile, :] = _flash_result(scr, qi, 0).astype(o_ref.dtype)


def _moba_kernel(qi_tab, t_tab, q_ref, k_ref, v_ref, kaux_ref, slope_ref, o_ref,
                 kx_scr, qx_scr, *sweep_refs, tile, n_q, n_blocks):
    blk = MOBA_BLOCK
    g = pl.program_id(1)
    scr = _SweepScratch.of(sweep_refs)
    seq = q_ref.shape[1]

    k = k_ref[0]
    kx_scr[:, :LANES] = k
    kx_scr[:, LANES:] = kaux_ref[...]
    kmean = jnp.sum(k.astype(F32).reshape(n_blocks, blk, LANES), axis=1) * (1.0 / blk)
    kmean_terms = _bf16_terms(kmean)

    q_raw = q_ref[0]
    q_scaled = (q_raw.astype(F32) * (HEAD_DIM ** -0.5 * LOG2E)).astype(q_raw.dtype)
    blk_id = lax.broadcasted_iota(jnp.int32, (n_blocks, blk), 0)
    row_id = lax.broadcasted_iota(jnp.int32, (LANES - n_blocks, seq), 0)
    for sub in range(2):
        head_mask = _head_lane_mask(sub)
        q_head = jnp.where(head_mask, q_raw, jnp.zeros_like(q_raw))
        gate = functools.reduce(
            lambda a, b: a + b, [_nt_dot(term, q_head) for term in kmean_terms])
        sel_cols = []
        for own in range(n_blocks):
            g_own = gate[:, own * blk:(own + 1) * blk]
            rank = jnp.zeros((n_blocks, blk), jnp.int32)
            for m in range(own):
                gm = g_own[m:m + 1, :]
                ahead = (gm > g_own) | ((gm == g_own) & (blk_id > m))
                rank = rank + ahead.astype(jnp.int32)
            allowed = ((rank < MOBA_TOPK) & (blk_id < own)) | (blk_id == own)
            sel_cols.append(jnp.where(allowed, 0.0, NEG))
        sel_t = jnp.concatenate(sel_cols, axis=1)
        alibi_t = jnp.zeros(row_id.shape, F32)
        for t in range(N_TERMS):
            term = slope_ref[pl.ds(t * MOBA_HEADS + 2 * g + sub, 1), :]
            term = jnp.concatenate([term] * (seq // LANES), axis=1)
            alibi_t = jnp.where(row_id == 2 * t, term * float(blk),
                                jnp.where(row_id == 2 * t + 1, term, alibi_t))
        aux_t = jnp.concatenate([sel_t, alibi_t], axis=0).astype(BF16)
        qx_scr[sub, :, :LANES] = jnp.where(head_mask, q_scaled, jnp.zeros_like(q_scaled))
        qx_scr[sub, :, LANES:] = _transpose_to_rows(aux_t).astype(qx_scr.dtype)

    _flash_sweep((qi_tab, t_tab),
                 lambda sub, start: qx_scr[sub, pl.ds(start, tile), :],
                 lambda sub, start: kx_scr[pl.ds(start, tile), :],
                 lambda sub, start: v_ref[0, pl.ds(start, tile), :],
                 scr, n_sub=2, tile=tile, n_q=n_q)
    _store_head_pair(o_ref, scr, tile=tile, n_q=n_q)


def _attention(kind, q_arr, k_arr, v_arr, extras, *, batch, seq, q_col, k_col, v_col):
    tile = ATTN_TILE
    n_q = seq // tile
    mla = kind == "mla"
    n_sub = 1 if mla else 2
    groups = MLA_HEADS if mla else FOX_HEADS // 2
    dk = QK_DEPTH if mla else LANES
    in_specs = [
        pl.BlockSpec((1, seq, dk), lambda b, g, *_: (b, 0, q_col + g)),
        pl.BlockSpec((1, seq, dk), lambda b, g, *_: (b, 0, k_col + g)),
        pl.BlockSpec((1, seq, LANES), lambda b, g, *_: (b, 0, v_col + g)),
    ]
    sweep_scr = _SweepScratch.shapes(n_sub, tile, n_q)
    operand_scr = [pltpu.VMEM((seq, QK_DEPTH), BF16), pltpu.VMEM((n_sub, seq, QK_DEPTH), BF16)]
    if kind == "fox":
        body = functools.partial(_fox_kernel, tile=tile, n_q=n_q)
        in_specs.append(pl.BlockSpec((1, seq, LANES), lambda b, g, *_: (b, 0, 0)))
        scratch = operand_scr + sweep_scr
    elif mla:
        body = functools.partial(_mla_kernel, tile=tile, n_q=n_q)
        scratch = sweep_scr
    else:
        body = functools.partial(_moba_kernel, tile=tile, n_q=n_q, n_blocks=seq // MOBA_BLOCK)
        in_specs.append(pl.BlockSpec((seq, LANES), lambda b, g, *_: (0, 0)))
        in_specs.append(pl.BlockSpec((N_TERMS * MOBA_HEADS, LANES), lambda b, g, *_: (0, 0)))
        scratch = operand_scr + sweep_scr
    return pl.pallas_call(
        body,
        grid_spec=pltpu.PrefetchScalarGridSpec(
            num_scalar_prefetch=2,
            grid=(batch, groups),
            in_specs=in_specs,
            out_specs=pl.BlockSpec((1, seq, LANES), lambda b, g, *_: (b, 0, g)),
            scratch_shapes=scratch),
        out_shape=jax.ShapeDtypeStruct((batch, seq, groups * LANES), BF16),
        compiler_params=pltpu.CompilerParams(
            dimension_semantics=("arbitrary", "arbitrary"),
            vmem_limit_bytes=VMEM_LIMIT_BYTES),
        name=kind + "_attn",
    )(*_sweep_items(n_q), q_arr, k_arr, v_arr, *extras)


def _combine_kernel(ya_ref, yb_ref, yc_ref, za_ref, zb_ref, zc_ref, x_ref, og_ref, w_ref, fg_ref,
                    o_ref, *, final):
    acc = x_ref[...]
    groups = ((ya_ref, za_ref), (yb_ref, zb_ref), (yc_ref, zc_ref))
    for idx, (y_ref, z_ref) in enumerate(groups):
        lo, hi = idx * FOX_WIDTH, (idx + 1) * FOX_WIDTH
        gated = _rms(y_ref[...].astype(F32), og_ref[:, lo:hi]) * z_ref[...].astype(F32)
        acc = acc + jnp.dot(gated.astype(BF16), w_ref[lo:hi, :], preferred_element_type=F32)
    if final:
        acc = _rms(acc, fg_ref[...])
    o_ref[...] = acc


def _combine(ya, yb, yc, proj, x2d, out_g, w_out, final_g, *, final):
    tokens = x2d.shape[0]
    tm = 512
    width = FOX_WIDTH
    zcol = lambda c: c * LANES // width
    y_spec = pl.BlockSpec((tm, width), lambda i: (i, 0))
    return pl.pallas_call(
        functools.partial(_combine_kernel, final=final),
        grid=(tokens // tm,),
        in_specs=[
            y_spec, y_spec, y_spec,
            pl.BlockSpec((tm, width), lambda i: (i, zcol(COL_FZ))),
            pl.BlockSpec((tm, width), lambda i: (i, zcol(COL_MZ))),
            pl.BlockSpec((tm, width), lambda i: (i, zcol(COL_BZ))),
            pl.BlockSpec((tm, D_MODEL), lambda i: (i, 0)),
            pl.BlockSpec((1, D_MIX), lambda i: (0, 0)),
            pl.BlockSpec((D_MIX, D_MODEL), lambda i: (0, 0)),
            pl.BlockSpec((1, D_MODEL), lambda i: (0, 0)),
        ],
        out_specs=pl.BlockSpec((tm, D_MODEL), lambda i: (i, 0)),
        out_shape=jax.ShapeDtypeStruct((tokens, D_MODEL), F32),
        compiler_params=pltpu.CompilerParams(
            dimension_semantics=("arbitrary",),
            vmem_limit_bytes=VMEM_LIMIT_BYTES),
        name="combine",
    )(ya, yb, yc, proj, proj, proj, x2d, out_g, w_out, final_g)


def _rot_cols(w):
    half = MLA_ROPE // 2
    return jnp.concatenate([-w[..., half:], w[..., :half]], axis=-1)


def _pack_layer(w_in, w_uq, w_ukv):
    cuts = np.cumsum(IN_SIZES)[:-1].tolist()
    fq, fk, fv, ff, fz, cq, ckv, kr, mz, bq, bk, bv, bz = jnp.split(w_in, cuts, axis=1)
    w_packed = jnp.concatenate(
        [fq, fk, fv, fz, bq, bk, bv, bz, mz, cq, ckv, kr, _rot_cols(kr)], axis=1).astype(BF16)
    wff_t = jnp.zeros((BF16_SUBLANES, D_MODEL), F32).at[:FOX_HEADS].set(ff.T).astype(BF16)
    wq = w_uq.reshape(MLA_Q_RANK, MLA_HEADS, MLA_QK)
    nope, rope = wq[..., :MLA_NOPE], wq[..., MLA_NOPE:]
    wq_packed = jnp.concatenate([nope, rope, _rot_cols(rope)], axis=-1)
    wq_packed = wq_packed.reshape(MLA_Q_RANK, MLA_HEADS * QK_DEPTH).astype(BF16)
    wkv = w_ukv.reshape(MLA_KV_RANK, MLA_HEADS, MLA_NOPE + MLA_V)
    wkv_packed = jnp.concatenate(
        [wkv[..., :MLA_NOPE].reshape(MLA_KV_RANK, MLA_WIDTH),
         wkv[..., MLA_NOPE:].reshape(MLA_KV_RANK, MLA_WIDTH)], axis=1).astype(BF16)
    return w_packed, wff_t, wq_packed, wkv_packed


def _moba_key_aux(seq):
    pos = np.arange(seq)
    aux = np.zeros((seq, LANES), np.float32)
    n_blocks = seq // MOBA_BLOCK
    aux[pos, pos // MOBA_BLOCK] = 1.0
    for t in range(N_TERMS):
        aux[:, n_blocks + 2 * t] = pos // MOBA_BLOCK
        aux[:, n_blocks + 2 * t + 1] = pos % MOBA_BLOCK
    return jnp.asarray(aux, BF16)


def kernel(x, ln_g, w_in, fox_b_f, mla_q_g, mla_w_uq, mla_kv_g, mla_w_ukv, out_g, w_out, final_g):
    batch, seq, _ = x.shape
    depth = ln_g.shape[0]
    tokens = batch * seq

    inv = ROPE_THETA ** (-jnp.arange(0, MLA_ROPE, 2, dtype=F32) / MLA_ROPE)
    ang = jnp.arange(seq, dtype=F32)[:, None] * inv[None, :]
    ang = jnp.concatenate([ang, ang], axis=-1)
    cs = jnp.concatenate([jnp.cos(ang), jnp.sin(ang)], axis=-1)
    slopes = 2.0 ** (-8.0 * jnp.arange(1, MOBA_HEADS + 1, dtype=F32) / MOBA_HEADS)
    slope_terms = jnp.concatenate([t.astype(F32) for t in _bf16_terms(slopes * LOG2E)])
    slope_terms = jnp.broadcast_to(slope_terms[:, None], (N_TERMS * MOBA_HEADS, LANES))
    moba_kaux = _moba_key_aux(seq)

    x2d = x.reshape(tokens, D_MODEL)
    for l in range(depth):
        w_packed, wff_t, wq_packed, wkv_packed = _pack_layer(w_in[l], mla_w_uq[l], mla_w_ukv[l])
        bf_col = jnp.zeros((BF16_SUBLANES, 1), F32).at[:FOX_HEADS, 0].set(fox_b_f[l])
        proj, caux = _inproj(x2d, ln_g[l][None, :], w_packed, wff_t, bf_col, seq=seq)
        q_cat, k_cat, v_mla = _mla_up(proj, cs, mla_q_g[l][None, :], mla_kv_g[l][None, :],
                                      wq_packed, wkv_packed, seq=seq)
        proj3 = proj.reshape(batch, seq, PACKED_COLS)
        ya = _attention("fox", proj3, proj3, proj3, [caux.reshape(batch, seq, LANES)],
                        batch=batch, seq=seq, q_col=COL_FQ, k_col=COL_FK, v_col=COL_FV)
        yb = _attention("mla", q_cat.reshape(batch, seq, -1), k_cat.reshape(batch, seq, -1),
                        v_mla.reshape(batch, seq, -1), [], batch=batch, seq=seq,
                        q_col=0, k_col=0, v_col=0)
        yc = _attention("moba", proj3, proj3, proj3, [moba_kaux, slope_terms],
                        batch=batch, seq=seq, q_col=COL_BQ, k_col=COL_BK, v_col=COL_BV)
        x2d = _combine(ya.reshape(tokens, -1), yb.reshape(tokens, -1), yc.reshape(tokens, -1),
                       proj, x2d, out_g[l][None, :], w_out[l].astype(BF16), final_g[None, :],
                       final=(l == depth - 1))
    return x2d.reshape(batch, seq, D_MODEL)
```

```python
import functools
from typing import Any, NamedTuple

import jax
import jax.numpy as jnp
import numpy as np
from jax import lax
from jax.experimental import pallas as pl
from jax.experimental.pallas import tpu as pltpu

D_MODEL = 1024
HEAD_DIM = 64
FOX_HEADS = 8
FOX_WIDTH = FOX_HEADS * HEAD_DIM
MLA_HEADS = 4
MLA_NOPE = 128
MLA_ROPE = 64
MLA_QK = MLA_NOPE + MLA_ROPE
MLA_V = 128
MLA_Q_RANK = 256
MLA_KV_RANK = 128
MLA_WIDTH = MLA_HEADS * MLA_V
MOBA_HEADS = 8
MOBA_WIDTH = MOBA_HEADS * HEAD_DIM
MOBA_BLOCK = 256
MOBA_TOPK = 3
D_MIX = FOX_WIDTH + MLA_WIDTH + MOBA_WIDTH
ROPE_THETA = 10000.0
EPS = 1e-6
IN_SIZES = (FOX_WIDTH, FOX_WIDTH, FOX_WIDTH, FOX_HEADS, FOX_WIDTH,
            MLA_Q_RANK, MLA_KV_RANK, MLA_ROPE, MLA_WIDTH,
            MOBA_WIDTH, MOBA_WIDTH, MOBA_WIDTH, MOBA_WIDTH)

LANES = 128
SUBLANES = 8
BF16_SUBLANES = 16
MXU_DEPTH = 256
VMEM_LIMIT_BYTES = 56 * 1024 * 1024

COL_FQ, COL_FK, COL_FV, COL_FZ = 0, 4, 8, 12
COL_BQ, COL_BK, COL_BV, COL_BZ = 16, 20, 24, 28
COL_MZ, COL_CQ, COL_CKV, COL_KR = 32, 36, 38, 39
PACKED_COLS = 40 * LANES
GATE_COLS = frozenset(c + k for c in (COL_FZ, COL_BZ, COL_MZ) for k in range(FOX_WIDTH // LANES))
QUERY_COLS = frozenset(c + k for c in (COL_FQ, COL_BQ) for k in range(FOX_WIDTH // LANES))

QK_DEPTH = MXU_DEPTH
N_TERMS = 3
NEG = -1e30
LOG2E = 1.4426950408889634
QUERY_SCALE = HEAD_DIM ** -0.5 * LOG2E
ATTN_TILE = 512
SWEEP_UNROLL = 12

F32 = jnp.float32
BF16 = jnp.bfloat16


def _rms(x, g):
    return x * lax.rsqrt(jnp.mean(x * x, axis=-1, keepdims=True) + EPS) * g


def _nt_dot(a, b):
    return lax.dot_general(a, b, (((1,), (1,)), ((), ())), preferred_element_type=F32)


def _bf16_terms(x):
    hi = x.astype(BF16)
    mid = (x - hi.astype(F32)).astype(BF16)
    lo = (x - hi.astype(F32) - mid.astype(F32)).astype(BF16)
    return hi, mid, lo


def _transpose_to_rows(cols_t):
    n = cols_t.shape[1]
    blk = MXU_DEPTH
    eye = (lax.broadcasted_iota(jnp.int32, (blk, blk), 0)
           == lax.broadcasted_iota(jnp.int32, (blk, blk), 1)).astype(BF16)
    return jnp.concatenate(
        [_nt_dot(eye, cols_t[:, s:s + blk]) for s in range(0, n, blk)], axis=0)


def _silu(z):
    return z / (1.0 + jnp.exp(-z))


def _inproj_kernel(x_ref, g_ref, w_ref, wff_ref, bf_ref, proj_ref, caux_ref, h_scr, carry_scr,
                   *, tiles_per_seq, tn):
    i = pl.program_id(0)

    @pl.when(i % tiles_per_seq == 0)
    def _():
        carry_scr[...] = jnp.zeros_like(carry_scr)

    h = _rms(x_ref[...], g_ref[...]).astype(BF16)
    h_scr[...] = h
    logit = _nt_dot(wff_ref[...], h) + bf_ref[...]

    for start in range(0, PACKED_COLS, tn):
        acc = jnp.dot(h_scr[...], w_ref[:, start:start + tn], preferred_element_type=F32)
        blocks = [acc[:, c0:c0 + LANES] for c0 in range(0, tn, LANES)]
        cols = [(start + k * LANES) // LANES for k in range(len(blocks))]
        blocks = [_silu(blk) if col in GATE_COLS else
                  blk * QUERY_SCALE if col in QUERY_COLS else blk
                  for col, blk in zip(cols, blocks)]
        proj_ref[:, start:start + tn] = jnp.concatenate(blocks, axis=1).astype(proj_ref.dtype)

    log_f = jnp.minimum(logit, 0.0) - jnp.log1p(jnp.exp(-jnp.abs(logit)))
    tm = log_f.shape[1]
    lane = lax.broadcasted_iota(jnp.int32, log_f.shape, 1)
    c = log_f
    d = 1
    while d < tm:
        c = c + jnp.where(lane >= d, pltpu.roll(c, d, axis=1), 0.0)
        d *= 2

    c = c + carry_scr[...][:, :1]
    carry_scr[...] = jnp.broadcast_to(c[:, tm - 1:tm], carry_scr.shape)
    terms = [t.astype(F32) for t in _bf16_terms(c[:FOX_HEADS] * LOG2E)]
    pad = jnp.zeros((LANES - N_TERMS * FOX_HEADS, tm), F32)
    caux_t = jnp.concatenate(terms + [pad], axis=0).astype(BF16)
    caux_ref[...] = _transpose_to_rows(caux_t).astype(caux_ref.dtype)


def _inproj(x2d, ln_g, w_packed, wff_t, bf_col, *, seq):
    tokens = x2d.shape[0]
    tm, tn = 512, 1024
    tiles_per_seq = seq // tm
    return pl.pallas_call(
        functools.partial(_inproj_kernel, tiles_per_seq=tiles_per_seq, tn=tn),
        grid=(tokens // tm,),
        in_specs=[
            pl.BlockSpec((tm, D_MODEL), lambda i: (i, 0)),
            pl.BlockSpec((1, D_MODEL), lambda i: (0, 0)),
            pl.BlockSpec((D_MODEL, PACKED_COLS), lambda i: (0, 0)),
            pl.BlockSpec((BF16_SUBLANES, D_MODEL), lambda i: (0, 0)),
            pl.BlockSpec((BF16_SUBLANES, 1), lambda i: (0, 0)),
        ],
        out_specs=[
            pl.BlockSpec((tm, PACKED_COLS), lambda i: (i, 0)),
            pl.BlockSpec((tm, LANES), lambda i: (i, 0)),
        ],
        out_shape=[
            jax.ShapeDtypeStruct((tokens, PACKED_COLS), BF16),
            jax.ShapeDtypeStruct((tokens, LANES), BF16),
        ],
        scratch_shapes=[
            pltpu.VMEM((tm, D_MODEL), BF16),
            pltpu.VMEM((BF16_SUBLANES, LANES), F32),
        ],
        compiler_params=pltpu.CompilerParams(
            dimension_semantics=("arbitrary",),
            vmem_limit_bytes=VMEM_LIMIT_BYTES),
        name="inproj",
    )(x2d, ln_g, w_packed, wff_t, bf_col)


def _mla_up_kernel(cq_ref, ckv_ref, krp_ref, cs_ref, gq_ref, gkv_ref, wq_ref, wkv_ref,
                   q_ref, k_ref, v_ref):
    lane = lax.broadcasted_iota(jnp.int32, (1, LANES), 1)
    cs = cs_ref[...]
    scale = MLA_QK ** -0.5 * LOG2E

    def rope_pair(a):
        t = a * cs
        return t + pltpu.roll(t, LANES // 2, axis=1)

    nq = _rms(cq_ref[...].astype(F32), gq_ref[...]).astype(BF16)
    qa = jnp.dot(nq, wq_ref[...], preferred_element_type=F32)
    for h in range(MLA_HEADS):
        base = h * QK_DEPTH
        q_ref[:, base:base + LANES] = (qa[:, base:base + LANES] * scale).astype(q_ref.dtype)
        rp = rope_pair(qa[:, base + LANES:base + QK_DEPTH])
        rp = jnp.where(lane < MLA_ROPE, rp, 0.0)
        q_ref[:, base + LANES:base + QK_DEPTH] = (rp * scale).astype(q_ref.dtype)

    nkv = _rms(ckv_ref[...].astype(F32), gkv_ref[...]).astype(BF16)
    kv = jnp.dot(nkv, wkv_ref[...], preferred_element_type=F32)
    kr = rope_pair(krp_ref[...].astype(F32)).astype(k_ref.dtype)
    for h in range(MLA_HEADS):
        base = h * QK_DEPTH
        k_ref[:, base:base + LANES] = kv[:, h * LANES:(h + 1) * LANES].astype(k_ref.dtype)
        k_ref[:, base + LANES:base + QK_DEPTH] = kr
    v_ref[...] = kv[:, MLA_HEADS * LANES:].astype(v_ref.dtype)


def _mla_up(proj, cs, g_q, g_kv, wq_packed, wkv_packed, *, seq):
    tokens = proj.shape[0]
    tm = 1024
    tiles_per_seq = seq // tm
    qk_cols = MLA_HEADS * QK_DEPTH
    return pl.pallas_call(
        _mla_up_kernel,
        grid=(tokens // tm,),
        in_specs=[
            pl.BlockSpec((tm, MLA_Q_RANK), lambda i: (i, COL_CQ * LANES // MLA_Q_RANK)),
            pl.BlockSpec((tm, LANES), lambda i: (i, COL_CKV)),
            pl.BlockSpec((tm, LANES), lambda i: (i, COL_KR)),
            pl.BlockSpec((tm, LANES), lambda i: (i % tiles_per_seq, 0)),
            pl.BlockSpec((1, MLA_Q_RANK), lambda i: (0, 0)),
            pl.BlockSpec((1, MLA_KV_RANK), lambda i: (0, 0)),
            pl.BlockSpec((MLA_Q_RANK, qk_cols), lambda i: (0, 0)),
            pl.BlockSpec((MLA_KV_RANK, 2 * MLA_WIDTH), lambda i: (0, 0)),
        ],
        out_specs=[
            pl.BlockSpec((tm, qk_cols), lambda i: (i, 0)),
            pl.BlockSpec((tm, qk_cols), lambda i: (i, 0)),
            pl.BlockSpec((tm, MLA_WIDTH), lambda i: (i, 0)),
        ],
        out_shape=[
            jax.ShapeDtypeStruct((tokens, qk_cols), BF16),
            jax.ShapeDtypeStruct((tokens, qk_cols), BF16),
            jax.ShapeDtypeStruct((tokens, MLA_WIDTH), BF16),
        ],
        compiler_params=pltpu.CompilerParams(
            dimension_semantics=("arbitrary",),
            vmem_limit_bytes=VMEM_LIMIT_BYTES),
        name="mla_up",
    )(proj, proj, proj, cs, g_q, g_kv, wq_packed, wkv_packed)


class _SweepScratch(NamedTuple):
    s: Any
    mx: Any
    p: Any
    alpha: Any
    m: Any
    l: Any
    acc: Any

    @classmethod
    def of(cls, refs):
        s0, s1, mx0, mx1, p0, p1, a0, a1, m, l, acc = refs
        return cls((s0, s1), (mx0, mx1), (p0, p1), (a0, a1), m, l, acc)

    @staticmethod
    def shapes(n_sub, tile, n_q):
        big = lambda dt: pltpu.VMEM((n_sub, tile, tile), dt)
        small = pltpu.VMEM((n_sub, tile, LANES), F32)
        state = pltpu.VMEM((n_q, n_sub, tile, LANES), F32)
        return [big(F32)] * 2 + [small] * 2 + [big(BF16)] * 2 + [small] * 2 + [state] * 3


def _sweep_items(n_q):
    items = [(i, i) for i in range(n_q)] + [(i, t) for i in range(n_q) for t in range(i)]
    return (jnp.asarray([i for i, _ in items], jnp.int32),
            jnp.asarray([t for _, t in items], jnp.int32))


def _chunk_max(s):
    return functools.reduce(
        jnp.maximum, [s[:, c:c + LANES] for c in range(0, s.shape[1], LANES)])


def _flash_sweep(tabs, q_rows, k_tile, v_tile, scr, *, n_sub, tile, n_q):
    qi_tab, t_tab = tabs
    row = lax.broadcasted_iota(jnp.int32, (tile, tile), 0)
    col = lax.broadcasted_iota(jnp.int32, (tile, tile), 1)
    causal = col <= row

    def logits(j, buf):
        q_start = pl.multiple_of(qi_tab[j] * tile, tile)
        k_start = pl.multiple_of(t_tab[j] * tile, tile)
        for sub in range(n_sub):
            s = _nt_dot(q_rows(sub, q_start), k_tile(sub, k_start))
            scr.s[buf][sub] = s
            scr.mx[buf][sub] = _chunk_max(s)

    def softmax(j, buf, diag):
        slot = qi_tab[j]
        for sub in range(n_sub):
            if diag:
                s = jnp.where(causal, scr.s[buf][sub], NEG)
                m_next = jnp.broadcast_to(jnp.max(_chunk_max(s), axis=1, keepdims=True),
                                          (tile, LANES))
            else:
                s = scr.s[buf][sub]
                m_prev = scr.m[slot, sub]
                m_next = jnp.maximum(m_prev, jnp.max(scr.mx[buf][sub], axis=1, keepdims=True))
                alpha = jnp.exp2(m_prev - m_next)
                scr.alpha[buf][sub] = alpha
            p = [jnp.exp2(s[:, c:c + LANES] - m_next) for c in range(0, tile, LANES)]
            p_sum = functools.reduce(lambda a, b: a + b, p)
            scr.l[slot, sub] = p_sum if diag else alpha * scr.l[slot, sub] + p_sum
            scr.m[slot, sub] = m_next
            scr.p[buf][sub] = jnp.concatenate(p, axis=1).astype(BF16)

    def weighted_values(j, buf, diag):
        slot = qi_tab[j]
        k_start = pl.multiple_of(t_tab[j] * tile, tile)
        for sub in range(n_sub):
            pv = jnp.dot(scr.p[buf][sub], v_tile(sub, k_start), preferred_element_type=F32)
            scr.acc[slot, sub] = pv if diag else scr.alpha[buf][sub] * scr.acc[slot, sub] + pv

    n_items = n_q * (n_q + 1) // 2
    tail = 4
    assert SWEEP_UNROLL % 2 == 0 and (n_items - n_q - tail) % SWEEP_UNROLL == 0

    def step(j, buf, *, ahead=True, attend=True):
        static = isinstance(j, int)
        weighted_values(j - 1, 1 - buf, static and j - 1 < n_q)
        if ahead:
            logits(j + 1, 1 - buf)
        if attend:
            softmax(j, buf, static and j < n_q)

    logits(0, 0)
    logits(1, 1)
    softmax(0, 0, True)
    for j in range(1, n_q + 1):
        step(j, j % 2)

    def steps(i, carry):
        for u in range(SWEEP_UNROLL):
            step(n_q + 1 + SWEEP_UNROLL * i + u, (n_q + 1 + u) % 2)
        return carry

    lax.fori_loop(0, (n_items - n_q - tail) // SWEEP_UNROLL, steps, 0)
    for j in range(n_items - tail + 1, n_items + 1):
        step(j, j % 2, ahead=j + 1 < n_items, attend=j < n_items)


def _flash_result(scr, qi, sub):
    return scr.acc[qi, sub] / jnp.sum(scr.l[qi, sub], axis=1, keepdims=True)


def _head_lane_mask(sub):
    lane = lax.broadcasted_iota(jnp.int32, (1, LANES), 1)
    return (lane < HEAD_DIM) if sub == 0 else (lane >= HEAD_DIM)


def _store_head_pair(o_ref, scr, *, tile, n_q):
    for qi in range(n_q):
        merged = jnp.where(_head_lane_mask(0), _flash_result(scr, qi, 0), _flash_result(scr, qi, 1))
        o_ref[0, qi * tile:(qi + 1) * tile, :] = merged.astype(o_ref.dtype)


def _fox_kernel(qi_tab, t_tab, q_ref, k_ref, v_ref, caux_ref, o_ref, kx_scr, qx_scr, *sweep_refs,
                tile, n_q):
    g = pl.program_id(1)
    scr = _SweepScratch.of(sweep_refs)
    kx_scr[:, :LANES] = k_ref[0]
    kx_scr[:, LANES:] = caux_ref[0]
    q = q_ref[0]
    lane = lax.broadcasted_iota(jnp.int32, (1, LANES), 1)
    for sub in range(2):
        head = 2 * g + sub
        qx_scr[sub, :, :LANES] = jnp.where(_head_lane_mask(sub), q, jnp.zeros_like(q))
        picks = functools.reduce(
            lambda a, b: a | b, [lane == head + t * FOX_HEADS for t in range(N_TERMS)])
        minus_ones = jnp.where(picks, -1.0, 0.0)
        qx_scr[sub, :, LANES:] = jnp.broadcast_to(minus_ones, q.shape).astype(qx_scr.dtype)

    _flash_sweep((qi_tab, t_tab),
                 lambda sub, start: qx_scr[sub, pl.ds(start, tile), :],
                 lambda sub, start: kx_scr[pl.ds(start, tile), :],
                 lambda sub, start: v_ref[0, pl.ds(start, tile), :],
                 scr, n_sub=2, tile=tile, n_q=n_q)
    _store_head_pair(o_ref, scr, tile=tile, n_q=n_q)


def _mla_kernel(qi_tab, t_tab, q_ref, k_ref, v_ref, o_ref, *sweep_refs, tile, n_q):
    scr = _SweepScratch.of(sweep_refs)
    _flash_sweep((qi_tab, t_tab),
                 lambda sub, start: q_ref[0, pl.ds(start, tile), :],
                 lambda sub, start: k_ref[0, pl.ds(start, tile), :],
                 lambda sub, start: v_ref[0, pl.ds(start, tile), :],
                 scr, n_sub=1, tile=tile, n_q=n_q)
    for qi in range(n_q):
        o_ref[0, qi * tile:(qi + 1) * tile, :] = _flash_result(scr, qi, 0).astype(o_ref.dtype)


def _moba_kernel(qi_tab, t_tab, q_ref, k_ref, v_ref, kaux_ref, slope_ref, o_ref,
                 kx_scr, qx_scr, *sweep_refs, tile, n_q, n_blocks):
    blk = MOBA_BLOCK
    g = pl.program_id(1)
    scr = _SweepScratch.of(sweep_refs)
    seq = q_ref.shape[1]

    k = k_ref[0]
    kx_scr[:, :LANES] = k
    kx_scr[:, LANES:] = kaux_ref[...]
    kmean = jnp.sum(k.astype(F32).reshape(n_blocks, blk, LANES), axis=1) * (1.0 / blk)
    kmean_terms = _bf16_terms(kmean)

    q = q_ref[0]
    blk_id = lax.broadcasted_iota(jnp.int32, (n_blocks, blk), 0)
    row_id = lax.broadcasted_iota(jnp.int32, (SUBLANES, seq), 0)
    for sub in range(2):
        head_mask = _head_lane_mask(sub)
        q_head = jnp.where(head_mask, q, jnp.zeros_like(q))
        gate = functools.reduce(
            lambda a, b: a + b, [_nt_dot(term, q_head) for term in kmean_terms])
        sel_cols = []
        for own in range(n_blocks):
            g_own = gate[:, own * blk:(own + 1) * blk]
            rank = jnp.zeros((n_blocks, blk), jnp.int32)
            for m in range(own):
                gm = g_own[m:m + 1, :]
                ahead = (gm > g_own) | ((gm == g_own) & (blk_id > m))
                rank = rank + ahead.astype(jnp.int32)
            allowed = ((rank < MOBA_TOPK) & (blk_id < own)) | (blk_id == own)
            sel_cols.append(jnp.where(allowed, 0.0, NEG))
        sel_t = jnp.concatenate(sel_cols, axis=1)
        alibi_t = jnp.zeros(row_id.shape, F32)
        for t in range(N_TERMS):
            term = slope_ref[pl.ds(t * MOBA_HEADS + 2 * g + sub, 1), :]
            term = jnp.concatenate([term] * (seq // LANES), axis=1)
            alibi_t = jnp.where(row_id == 2 * t, term * float(blk),
                                jnp.where(row_id == 2 * t + 1, term, alibi_t))
        pad = jnp.zeros((LANES - n_blocks - SUBLANES, seq), F32)
        aux_t = jnp.concatenate([sel_t, alibi_t, pad], axis=0).astype(BF16)
        qx_scr[sub, :, :LANES] = q_head
        qx_scr[sub, :, LANES:] = _transpose_to_rows(aux_t).astype(qx_scr.dtype)

    _flash_sweep((qi_tab, t_tab),
                 lambda sub, start: qx_scr[sub, pl.ds(start, tile), :],
                 lambda sub, start: kx_scr[pl.ds(start, tile), :],
                 lambda sub, start: v_ref[0, pl.ds(start, tile), :],
                 scr, n_sub=2, tile=tile, n_q=n_q)
    _store_head_pair(o_ref, scr, tile=tile, n_q=n_q)


def _attention(kind, q_arr, k_arr, v_arr, extras, *, batch, seq, q_col, k_col, v_col):
    tile = ATTN_TILE
    n_q = seq // tile
    mla = kind == "mla"
    n_sub = 1 if mla else 2
    groups = MLA_HEADS if mla else FOX_HEADS // 2
    dk = QK_DEPTH if mla else LANES
    in_specs = [
        pl.BlockSpec((1, seq, dk), lambda b, g, *_: (b, 0, q_col + g)),
        pl.BlockSpec((1, seq, dk), lambda b, g, *_: (b, 0, k_col + g)),
        pl.BlockSpec((1, seq, LANES), lambda b, g, *_: (b, 0, v_col + g)),
    ]
    sweep_scr = _SweepScratch.shapes(n_sub, tile, n_q)
    operand_scr = [pltpu.VMEM((seq, QK_DEPTH), BF16), pltpu.VMEM((n_sub, seq, QK_DEPTH), BF16)]
    if kind == "fox":
        body = functools.partial(_fox_kernel, tile=tile, n_q=n_q)
        in_specs.append(pl.BlockSpec((1, seq, LANES), lambda b, g, *_: (b, 0, 0)))
        scratch = operand_scr + sweep_scr
    elif mla:
        body = functools.partial(_mla_kernel, tile=tile, n_q=n_q)
        scratch = sweep_scr
    else:
        body = functools.partial(_moba_kernel, tile=tile, n_q=n_q, n_blocks=seq // MOBA_BLOCK)
        in_specs.append(pl.BlockSpec((seq, LANES), lambda b, g, *_: (0, 0)))
        in_specs.append(pl.BlockSpec((N_TERMS * MOBA_HEADS, LANES), lambda b, g, *_: (0, 0)))
        scratch = operand_scr + sweep_scr
    return pl.pallas_call(
        body,
        grid_spec=pltpu.PrefetchScalarGridSpec(
            num_scalar_prefetch=2,
            grid=(batch, groups),
            in_specs=in_specs,
            out_specs=pl.BlockSpec((1, seq, LANES), lambda b, g, *_: (b, 0, g)),
            scratch_shapes=scratch),
        out_shape=jax.ShapeDtypeStruct((batch, seq, groups * LANES), BF16),
        compiler_params=pltpu.CompilerParams(
            dimension_semantics=("arbitrary", "arbitrary"),
            vmem_limit_bytes=VMEM_LIMIT_BYTES),
        name=kind + "_attn",
    )(*_sweep_items(n_q), q_arr, k_arr, v_arr, *extras)


def _combine_kernel(ya_ref, yb_ref, yc_ref, za_ref, zb_ref, zc_ref, x_ref, og_ref, w_ref, fg_ref,
                    o_ref, *, final):
    acc = x_ref[...]
    groups = ((ya_ref, za_ref), (yb_ref, zb_ref), (yc_ref, zc_ref))
    for idx, (y_ref, z_ref) in enumerate(groups):
        lo, hi = idx * FOX_WIDTH, (idx + 1) * FOX_WIDTH
        gated = _rms(y_ref[...].astype(F32), og_ref[:, lo:hi]) * z_ref[...].astype(F32)
        acc = acc + jnp.dot(gated.astype(BF16), w_ref[lo:hi, :], preferred_element_type=F32)
    if final:
        acc = _rms(acc, fg_ref[...])
    o_ref[...] = acc


def _combine(ya, yb, yc, proj, x2d, out_g, w_out, final_g, *, final):
    tokens = x2d.shape[0]
    tm = 512
    width = FOX_WIDTH
    zcol = lambda c: c * LANES // width
    y_spec = pl.BlockSpec((tm, width), lambda i: (i, 0))
    return pl.pallas_call(
        functools.partial(_combine_kernel, final=final),
        grid=(tokens // tm,),
        in_specs=[
            y_spec, y_spec, y_spec,
            pl.BlockSpec((tm, width), lambda i: (i, zcol(COL_FZ))),
            pl.BlockSpec((tm, width), lambda i: (i, zcol(COL_MZ))),
            pl.BlockSpec((tm, width), lambda i: (i, zcol(COL_BZ))),
            pl.BlockSpec((tm, D_MODEL), lambda i: (i, 0)),
            pl.BlockSpec((1, D_MIX), lambda i: (0, 0)),
            pl.BlockSpec((D_MIX, D_MODEL), lambda i: (0, 0)),
            pl.BlockSpec((1, D_MODEL), lambda i: (0, 0)),
        ],
        out_specs=pl.BlockSpec((tm, D_MODEL), lambda i: (i, 0)),
        out_shape=jax.ShapeDtypeStruct((tokens, D_MODEL), F32),
        compiler_params=pltpu.CompilerParams(
            dimension_semantics=("arbitrary",),
            vmem_limit_bytes=VMEM_LIMIT_BYTES),
        name="combine",
    )(ya, yb, yc, proj, proj, proj, x2d, out_g, w_out, final_g)


def _rot_cols(w):
    half = MLA_ROPE // 2
    return jnp.concatenate([-w[..., half:], w[..., :half]], axis=-1)


def _pack_layer(w_in, w_uq, w_ukv):
    cuts = np.cumsum(IN_SIZES)[:-1].tolist()
    fq, fk, fv, ff, fz, cq, ckv, kr, mz, bq, bk, bv, bz = jnp.split(w_in, cuts, axis=1)
    w_packed = jnp.concatenate(
        [fq, fk, fv, fz, bq, bk, bv, bz, mz, cq, ckv, kr, _rot_cols(kr)], axis=1).astype(BF16)
    wff_t = jnp.zeros((BF16_SUBLANES, D_MODEL), F32).at[:FOX_HEADS].set(ff.T).astype(BF16)
    wq = w_uq.reshape(MLA_Q_RANK, MLA_HEADS, MLA_QK)
    nope, rope = wq[..., :MLA_NOPE], wq[..., MLA_NOPE:]
    wq_packed = jnp.concatenate([nope, rope, _rot_cols(rope)], axis=-1)
    wq_packed = wq_packed.reshape(MLA_Q_RANK, MLA_HEADS * QK_DEPTH).astype(BF16)
    wkv = w_ukv.reshape(MLA_KV_RANK, MLA_HEADS, MLA_NOPE + MLA_V)
    wkv_packed = jnp.concatenate(
        [wkv[..., :MLA_NOPE].reshape(MLA_KV_RANK, MLA_WIDTH),
         wkv[..., MLA_NOPE:].reshape(MLA_KV_RANK, MLA_WIDTH)], axis=1).astype(BF16)
    return w_packed, wff_t, wq_packed, wkv_packed


def _moba_key_aux(seq):
    pos = np.arange(seq)
    aux = np.zeros((seq, LANES), np.float32)
    n_blocks = seq // MOBA_BLOCK
    aux[pos, pos // MOBA_BLOCK] = 1.0
    for t in range(N_TERMS):
        aux[:, n_blocks + 2 * t] = pos // MOBA_BLOCK
        aux[:, n_blocks + 2 * t + 1] = pos % MOBA_BLOCK
    return jnp.asarray(aux, BF16)


def kernel(x, ln_g, w_in, fox_b_f, mla_q_g, mla_w_uq, mla_kv_g, mla_w_ukv, out_g, w_out, final_g):
    batch, seq, _ = x.shape
    depth = ln_g.shape[0]
    tokens = batch * seq

    inv = ROPE_THETA ** (-jnp.arange(0, MLA_ROPE, 2, dtype=F32) / MLA_ROPE)
    ang = jnp.arange(seq, dtype=F32)[:, None] * inv[None, :]
    ang = jnp.concatenate([ang, ang], axis=-1)
    cs = jnp.concatenate([jnp.cos(ang), jnp.sin(ang)], axis=-1)
    slopes = 2.0 ** (-8.0 * jnp.arange(1, MOBA_HEADS + 1, dtype=F32) / MOBA_HEADS)
    slope_terms = jnp.concatenate([t.astype(F32) for t in _bf16_terms(slopes * LOG2E)])
    slope_terms = jnp.broadcast_to(slope_terms[:, None], (N_TERMS * MOBA_HEADS, LANES))
    moba_kaux = _moba_key_aux(seq)

    x2d = x.reshape(tokens, D_MODEL)
    for l in range(depth):
        w_packed, wff_t, wq_packed, wkv_packed = _pack_layer(w_in[l], mla_w_uq[l], mla_w_ukv[l])
        bf_col = jnp.zeros((BF16_SUBLANES, 1), F32).at[:FOX_HEADS, 0].set(fox_b_f[l])
        proj, caux = _inproj(x2d, ln_g[l][None, :], w_packed, wff_t, bf_col, seq=seq)
        q_cat, k_cat, v_mla = _mla_up(proj, cs, mla_q_g[l][None, :], mla_kv_g[l][None, :],
                                      wq_packed, wkv_packed, seq=seq)
        proj3 = proj.reshape(batch, seq, PACKED_COLS)
        ya = _attention("fox", proj3, proj3, proj3, [caux.reshape(batch, seq, LANES)],
                        batch=batch, seq=seq, q_col=COL_FQ, k_col=COL_FK, v_col=COL_FV)
        yb = _attention("mla", q_cat.reshape(batch, seq, -1), k_cat.reshape(batch, seq, -1),
                        v_mla.reshape(batch, seq, -1), [], batch=batch, seq=seq,
                        q_col=0, k_col=0, v_col=0)
        yc = _attention("moba", proj3, proj3, proj3, [moba_kaux, slope_terms],
                        batch=batch, seq=seq, q_col=COL_BQ, k_col=COL_BK, v_col=COL_BV)
        x2d = _combine(ya.reshape(tokens, -1), yb.reshape(tokens, -1), yc.reshape(tokens, -1),
                       proj, x2d, out_g[l][None, :], w_out[l].astype(BF16), final_g[None, :],
                       final=(l == depth - 1))
    return x2d.reshape(batch, seq, D_MODEL)
```

```python
import functools
from typing import Any, NamedTuple

import jax
import jax.numpy as jnp
import numpy as np
from jax import lax
from jax.experimental import pallas as pl
from jax.experimental.pallas import tpu as pltpu

D_MODEL = 1024
HEAD_DIM = 64
FOX_HEADS = 8
FOX_WIDTH = FOX_HEADS * HEAD_DIM
MLA_HEADS = 4
MLA_NOPE = 128
MLA_ROPE = 64
MLA_QK = MLA_NOPE + MLA_ROPE
MLA_V = 128
MLA_Q_RANK = 256
MLA_KV_RANK = 128
MLA_WIDTH = MLA_HEADS * MLA_V
MOBA_HEADS = 8
MOBA_WIDTH = MOBA_HEADS * HEAD_DIM
MOBA_BLOCK = 256
MOBA_TOPK = 3
D_MIX = FOX_WIDTH + MLA_WIDTH + MOBA_WIDTH
ROPE_THETA = 10000.0
EPS = 1e-6
IN_SIZES = (FOX_WIDTH, FOX_WIDTH, FOX_WIDTH, FOX_HEADS, FOX_WIDTH,
            MLA_Q_RANK, MLA_KV_RANK, MLA_ROPE, MLA_WIDTH,
            MOBA_WIDTH, MOBA_WIDTH, MOBA_WIDTH, MOBA_WIDTH)

LANES = 128
SUBLANES = 8
BF16_SUBLANES = 16
MXU_DEPTH = 256
VMEM_LIMIT_BYTES = 56 * 1024 * 1024

COL_FQ, COL_FK, COL_FV, COL_FZ = 0, 4, 8, 12
COL_BQ, COL_BK, COL_BV, COL_BZ = 16, 20, 24, 28
COL_MZ, COL_CQ, COL_CKV, COL_KR = 32, 36, 38, 39
PACKED_COLS = 40 * LANES
GATE_COLS = frozenset(c + k for c in (COL_FZ, COL_BZ, COL_MZ) for k in range(FOX_WIDTH // LANES))
QUERY_COLS = frozenset(c + k for c in (COL_FQ, COL_BQ) for k in range(FOX_WIDTH // LANES))

QK_DEPTH = MXU_DEPTH
N_TERMS = 3
NEG = -1e30
LOG2E = 1.4426950408889634
QUERY_SCALE = HEAD_DIM ** -0.5 * LOG2E
ATTN_TILE = 512
SWEEP_UNROLL = 12
INPROJ_ROWS, INPROJ_COLS = 512, 1024
MLA_UP_ROWS = 1024
COMBINE_ROWS = 512

F32 = jnp.float32
BF16 = jnp.bfloat16


def _rms(x, g):
    return x * lax.rsqrt(jnp.mean(x * x, axis=-1, keepdims=True) + EPS) * g


def _nt_dot(a, b):
    return lax.dot_general(a, b, (((1,), (1,)), ((), ())), preferred_element_type=F32)


def _bf16_terms(x):
    hi = x.astype(BF16)
    mid = (x - hi.astype(F32)).astype(BF16)
    lo = (x - hi.astype(F32) - mid.astype(F32)).astype(BF16)
    return hi, mid, lo


def _transpose_to_rows(cols_t):
    n = cols_t.shape[1]
    blk = MXU_DEPTH
    eye = (lax.broadcasted_iota(jnp.int32, (blk, blk), 0)
           == lax.broadcasted_iota(jnp.int32, (blk, blk), 1)).astype(BF16)
    return jnp.concatenate(
        [_nt_dot(eye, cols_t[:, s:s + blk]) for s in range(0, n, blk)], axis=0)


def _silu(z):
    return z / (1.0 + jnp.exp(-z))


def _inproj_kernel(x_ref, g_ref, w_ref, wff_ref, bf_ref, proj_ref, caux_ref, h_scr, carry_scr,
                   *, tiles_per_seq, tn):
    i = pl.program_id(0)

    @pl.when(i % tiles_per_seq == 0)
    def _():
        carry_scr[...] = jnp.zeros_like(carry_scr)

    h = _rms(x_ref[...], g_ref[...]).astype(BF16)
    h_scr[...] = h
    logit = _nt_dot(wff_ref[...], h) + bf_ref[...]

    for start in range(0, PACKED_COLS, tn):
        acc = jnp.dot(h_scr[...], w_ref[:, start:start + tn], preferred_element_type=F32)
        blocks = [acc[:, c0:c0 + LANES] for c0 in range(0, tn, LANES)]
        cols = [(start + k * LANES) // LANES for k in range(len(blocks))]
        blocks = [_silu(blk) if col in GATE_COLS else
                  blk * QUERY_SCALE if col in QUERY_COLS else blk
                  for col, blk in zip(cols, blocks)]
        proj_ref[:, start:start + tn] = jnp.concatenate(blocks, axis=1).astype(proj_ref.dtype)

    log_f = jnp.minimum(logit, 0.0) - jnp.log1p(jnp.exp(-jnp.abs(logit)))
    tm = log_f.shape[1]
    lane = lax.broadcasted_iota(jnp.int32, log_f.shape, 1)
    c = log_f
    d = 1
    while d < tm:
        c = c + jnp.where(lane >= d, pltpu.roll(c, d, axis=1), 0.0)
        d *= 2

    c = c + carry_scr[...][:, :1]
    carry_scr[...] = jnp.broadcast_to(c[:, tm - 1:tm], carry_scr.shape)
    terms = [t.astype(F32) for t in _bf16_terms(c[:FOX_HEADS] * LOG2E)]
    pad = jnp.zeros((LANES - N_TERMS * FOX_HEADS, tm), F32)
    caux_t = jnp.concatenate(terms + [pad], axis=0).astype(BF16)
    caux_ref[...] = _transpose_to_rows(caux_t).astype(caux_ref.dtype)


def _inproj(x2d, ln_g, w_packed, wff_t, bf_col, *, seq):
    tokens = x2d.shape[0]
    tm, tn = INPROJ_ROWS, INPROJ_COLS
    tiles_per_seq = seq // tm
    return pl.pallas_call(
        functools.partial(_inproj_kernel, tiles_per_seq=tiles_per_seq, tn=tn),
        grid=(tokens // tm,),
        in_specs=[
            pl.BlockSpec((tm, D_MODEL), lambda i: (i, 0)),
            pl.BlockSpec((1, D_MODEL), lambda i: (0, 0)),
            pl.BlockSpec((D_MODEL, PACKED_COLS), lambda i: (0, 0)),
            pl.BlockSpec((BF16_SUBLANES, D_MODEL), lambda i: (0, 0)),
            pl.BlockSpec((BF16_SUBLANES, 1), lambda i: (0, 0)),
        ],
        out_specs=[
            pl.BlockSpec((tm, PACKED_COLS), lambda i: (i, 0)),
            pl.BlockSpec((tm, LANES), lambda i: (i, 0)),
        ],
        out_shape=[
            jax.ShapeDtypeStruct((tokens, PACKED_COLS), BF16),
            jax.ShapeDtypeStruct((tokens, LANES), BF16),
        ],
        scratch_shapes=[
            pltpu.VMEM((tm, D_MODEL), BF16),
            pltpu.VMEM((BF16_SUBLANES, LANES), F32),
        ],
        compiler_params=pltpu.CompilerParams(
            dimension_semantics=("arbitrary",),
            vmem_limit_bytes=VMEM_LIMIT_BYTES),
        name="inproj",
    )(x2d, ln_g, w_packed, wff_t, bf_col)


def _mla_up_kernel(cq_ref, ckv_ref, krp_ref, cs_ref, gq_ref, gkv_ref, wq_ref, wkv_ref,
                   q_ref, k_ref, v_ref):
    lane = lax.broadcasted_iota(jnp.int32, (1, LANES), 1)
    cs = cs_ref[...]
    scale = MLA_QK ** -0.5 * LOG2E

    def rope_pair(a):
        t = a * cs
        return t + pltpu.roll(t, LANES // 2, axis=1)

    nq = _rms(cq_ref[...].astype(F32), gq_ref[...]).astype(BF16)
    qa = jnp.dot(nq, wq_ref[...], preferred_element_type=F32)
    for h in range(MLA_HEADS):
        base = h * QK_DEPTH
        q_ref[:, base:base + LANES] = (qa[:, base:base + LANES] * scale).astype(q_ref.dtype)
        rp = rope_pair(qa[:, base + LANES:base + QK_DEPTH])
        rp = jnp.where(lane < MLA_ROPE, rp, 0.0)
        q_ref[:, base + LANES:base + QK_DEPTH] = (rp * scale).astype(q_ref.dtype)

    nkv = _rms(ckv_ref[...].astype(F32), gkv_ref[...]).astype(BF16)
    kv = jnp.dot(nkv, wkv_ref[...], preferred_element_type=F32)
    kr = rope_pair(krp_ref[...].astype(F32)).astype(k_ref.dtype)
    for h in range(MLA_HEADS):
        base = h * QK_DEPTH
        k_ref[:, base:base + LANES] = kv[:, h * LANES:(h + 1) * LANES].astype(k_ref.dtype)
        k_ref[:, base + LANES:base + QK_DEPTH] = kr
    v_ref[...] = kv[:, MLA_HEADS * LANES:].astype(v_ref.dtype)


def _mla_up(proj, cs, g_q, g_kv, wq_packed, wkv_packed, *, seq):
    tokens = proj.shape[0]
    tm = MLA_UP_ROWS
    tiles_per_seq = seq // tm
    qk_cols = MLA_HEADS * QK_DEPTH
    return pl.pallas_call(
        _mla_up_kernel,
        grid=(tokens // tm,),
        in_specs=[
            pl.BlockSpec((tm, MLA_Q_RANK), lambda i: (i, COL_CQ * LANES // MLA_Q_RANK)),
            pl.BlockSpec((tm, LANES), lambda i: (i, COL_CKV)),
            pl.BlockSpec((tm, LANES), lambda i: (i, COL_KR)),
            pl.BlockSpec((tm, LANES), lambda i: (i % tiles_per_seq, 0)),
            pl.BlockSpec((1, MLA_Q_RANK), lambda i: (0, 0)),
            pl.BlockSpec((1, MLA_KV_RANK), lambda i: (0, 0)),
            pl.BlockSpec((MLA_Q_RANK, qk_cols), lambda i: (0, 0)),
            pl.BlockSpec((MLA_KV_RANK, 2 * MLA_WIDTH), lambda i: (0, 0)),
        ],
        out_specs=[
            pl.BlockSpec((tm, qk_cols), lambda i: (i, 0)),
            pl.BlockSpec((tm, qk_cols), lambda i: (i, 0)),
            pl.BlockSpec((tm, MLA_WIDTH), lambda i: (i, 0)),
        ],
        out_shape=[
            jax.ShapeDtypeStruct((tokens, qk_cols), BF16),
            jax.ShapeDtypeStruct((tokens, qk_cols), BF16),
            jax.ShapeDtypeStruct((tokens, MLA_WIDTH), BF16),
        ],
        compiler_params=pltpu.CompilerParams(
            dimension_semantics=("arbitrary",),
            vmem_limit_bytes=VMEM_LIMIT_BYTES),
        name="mla_up",
    )(proj, proj, proj, cs, g_q, g_kv, wq_packed, wkv_packed)


class _SweepScratch(NamedTuple):
    s: Any
    mx: Any
    p: Any
    alpha: Any
    m: Any
    l: Any
    acc: Any

    @classmethod
    def of(cls, refs):
        s0, s1, mx0, mx1, p0, p1, a0, a1, m, l, acc = refs
        return cls((s0, s1), (mx0, mx1), (p0, p1), (a0, a1), m, l, acc)

    @staticmethod
    def shapes(n_sub, tile, n_q):
        big = lambda dt: pltpu.VMEM((n_sub, tile, tile), dt)
        small = pltpu.VMEM((n_sub, tile, LANES), F32)
        state = pltpu.VMEM((n_q, n_sub, tile, LANES), F32)
        return [big(F32)] * 2 + [small] * 2 + [big(BF16)] * 2 + [small] * 2 + [state] * 3


def _sweep_items(n_q):
    items = [(i, i) for i in range(n_q)] + [(i, t) for i in range(n_q) for t in range(i)]
    return (jnp.asarray([i for i, _ in items], jnp.int32),
            jnp.asarray([t for _, t in items], jnp.int32))


def _chunk_max(s):
    return functools.reduce(
        jnp.maximum, [s[:, c:c + LANES] for c in range(0, s.shape[1], LANES)])


def _flash_sweep(tabs, q_rows, k_tile, v_tile, scr, *, n_sub, tile, n_q):
    qi_tab, t_tab = tabs
    row = lax.broadcasted_iota(jnp.int32, (tile, tile), 0)
    col = lax.broadcasted_iota(jnp.int32, (tile, tile), 1)
    causal = col <= row

    def logits(j, buf):
        q_start = pl.multiple_of(qi_tab[j] * tile, tile)
        k_start = pl.multiple_of(t_tab[j] * tile, tile)
        for sub in range(n_sub):
            s = _nt_dot(q_rows(sub, q_start), k_tile(sub, k_start))
            scr.s[buf][sub] = s
            scr.mx[buf][sub] = _chunk_max(s)

    def softmax(j, buf, diag):
        slot = qi_tab[j]
        for sub in range(n_sub):
            if diag:
                s = jnp.where(causal, scr.s[buf][sub], NEG)
                m_next = jnp.broadcast_to(jnp.max(_chunk_max(s), axis=1, keepdims=True),
                                          (tile, LANES))
            else:
                s = scr.s[buf][sub]
                m_prev = scr.m[slot, sub]
                m_next = jnp.maximum(m_prev, jnp.max(scr.mx[buf][sub], axis=1, keepdims=True))
                alpha = jnp.exp2(m_prev - m_next)
                scr.alpha[buf][sub] = alpha
            p = [jnp.exp2(s[:, c:c + LANES] - m_next) for c in range(0, tile, LANES)]
            p_sum = functools.reduce(lambda a, b: a + b, p)
            scr.l[slot, sub] = p_sum if diag else alpha * scr.l[slot, sub] + p_sum
            scr.m[slot, sub] = m_next
            scr.p[buf][sub] = jnp.concatenate(p, axis=1).astype(BF16)

    def weighted_values(j, buf, diag):
        slot = qi_tab[j]
        k_start = pl.multiple_of(t_tab[j] * tile, tile)
        for sub in range(n_sub):
            pv = jnp.dot(scr.p[buf][sub], v_tile(sub, k_start), preferred_element_type=F32)
            scr.acc[slot, sub] = pv if diag else scr.alpha[buf][sub] * scr.acc[slot, sub] + pv

    n_items = n_q * (n_q + 1) // 2
    tail = 4
    assert SWEEP_UNROLL % 2 == 0 and (n_items - n_q - tail) % SWEEP_UNROLL == 0

    def step(j, buf, *, ahead=True, attend=True):
        static = isinstance(j, int)
        weighted_values(j - 1, 1 - buf, static and j - 1 < n_q)
        if ahead:
            logits(j + 1, 1 - buf)
        if attend:
            softmax(j, buf, static and j < n_q)

    logits(0, 0)
    logits(1, 1)
    softmax(0, 0, True)
    for j in range(1, n_q + 1):
        step(j, j % 2)

    def steps(i, carry):
        for u in range(SWEEP_UNROLL):
            step(n_q + 1 + SWEEP_UNROLL * i + u, (n_q + 1 + u) % 2)
        return carry

    lax.fori_loop(0, (n_items - n_q - tail) // SWEEP_UNROLL, steps, 0)
    for j in range(n_items - tail + 1, n_items + 1):
        step(j, j % 2, ahead=j + 1 < n_items, attend=j < n_items)


def _flash_result(scr, qi, sub):
    return scr.acc[qi, sub] / jnp.sum(scr.l[qi, sub], axis=1, keepdims=True)


def _head_lane_mask(sub):
    lane = lax.broadcasted_iota(jnp.int32, (1, LANES), 1)
    return (lane < HEAD_DIM) if sub == 0 else (lane >= HEAD_DIM)


def _store_head_pair(o_ref, scr, *, tile, n_q):
    first = _head_lane_mask(0)
    for qi in range(n_q):
        acc = jnp.where(first, scr.acc[qi, 0], scr.acc[qi, 1])
        row_sum = jnp.where(first, jnp.sum(scr.l[qi, 0], axis=1, keepdims=True),
                            jnp.sum(scr.l[qi, 1], axis=1, keepdims=True))
        o_ref[0, qi * tile:(qi + 1) * tile, :] = (acc / row_sum).astype(o_ref.dtype)


def _fox_kernel(qi_tab, t_tab, q_ref, k_ref, v_ref, caux_ref, o_ref, kx_scr, qx_scr, *sweep_refs,
                tile, n_q):
    g = pl.program_id(1)
    scr = _SweepScratch.of(sweep_refs)
    kx_scr[:, :LANES] = k_ref[0]
    kx_scr[:, LANES:] = caux_ref[0]
    q = q_ref[0]
    lane = lax.broadcasted_iota(jnp.int32, (1, LANES), 1)
    for sub in range(2):
        head = 2 * g + sub
        qx_scr[sub, :, :LANES] = jnp.where(_head_lane_mask(sub), q, jnp.zeros_like(q))
        picks = functools.reduce(
            lambda a, b: a | b, [lane == head + t * FOX_HEADS for t in range(N_TERMS)])
        minus_ones = jnp.where(picks, -1.0, 0.0)
        qx_scr[sub, :, LANES:] = jnp.broadcast_to(minus_ones, q.shape).astype(qx_scr.dtype)

    _flash_sweep((qi_tab, t_tab),
                 lambda sub, start: qx_scr[sub, pl.ds(start, tile), :],
                 lambda sub, start: kx_scr[pl.ds(start, tile), :],
                 lambda sub, start: v_ref[0, pl.ds(start, tile), :],
                 scr, n_sub=2, tile=tile, n_q=n_q)
    _store_head_pair(o_ref, scr, tile=tile, n_q=n_q)


def _mla_kernel(qi_tab, t_tab, q_ref, k_ref, v_ref, o_ref, *sweep_refs, tile, n_q):
    scr = _SweepScratch.of(sweep_refs)
    _flash_sweep((qi_tab, t_tab),
                 lambda sub, start: q_ref[0, pl.ds(start, tile), :],
                 lambda sub, start: k_ref[0, pl.ds(start, tile), :],
                 lambda sub, start: v_ref[0, pl.ds(start, tile), :],
                 scr, n_sub=1, tile=tile, n_q=n_q)
    for qi in range(n_q):
        o_ref[0, qi * tile:(qi + 1) * tile, :] = _flash_result(scr, qi, 0).astype(o_ref.dtype)


def _moba_kernel(qi_tab, t_tab, q_ref, k_ref, v_ref, kaux_ref, slope_ref, o_ref,
                 kx_scr, qx_scr, *sweep_refs, tile, n_q, n_blocks):
    blk = MOBA_BLOCK
    g = pl.program_id(1)
    scr = _SweepScratch.of(sweep_refs)
    seq = q_ref.shape[1]

    k = k_ref[0]
    kx_scr[:, :LANES] = k
    kx_scr[:, LANES:] = kaux_ref[...]
    kmean = jnp.sum(k.astype(F32).reshape(n_blocks, blk, LANES), axis=1) * (1.0 / blk)
    kmean_terms = _bf16_terms(kmean)

    q = q_ref[0]
    blk_id = lax.broadcasted_iota(jnp.int32, (n_blocks, blk), 0)
    row_id = lax.broadcasted_iota(jnp.int32, (SUBLANES, seq), 0)
    for sub in range(2):
        head_mask = _head_lane_mask(sub)
        q_head = jnp.where(head_mask, q, jnp.zeros_like(q))
        gate = functools.reduce(
            lambda a, b: a + b, [_nt_dot(term, q_head) for term in kmean_terms])
        sel_cols = []
        for own in range(n_blocks):
            g_own = gate[:, own * blk:(own + 1) * blk]
            rank = jnp.zeros((n_blocks, blk), jnp.int32)
            for m in range(own):
                gm = g_own[m:m + 1, :]
                ahead = (gm > g_own) | ((gm == g_own) & (blk_id > m))
                rank = rank + ahead.astype(jnp.int32)
            allowed = ((rank < MOBA_TOPK) & (blk_id < own)) | (blk_id == own)
            sel_cols.append(jnp.where(allowed, 0.0, NEG))
        sel_t = jnp.concatenate(sel_cols, axis=1)
        alibi_t = jnp.zeros(row_id.shape, F32)
        for t in range(N_TERMS):
            term = slope_ref[pl.ds(t * MOBA_HEADS + 2 * g + sub, 1), :]
            term = jnp.concatenate([term] * (seq // LANES), axis=1)
            alibi_t = jnp.where(row_id == 2 * t, term * float(blk),
                                jnp.where(row_id == 2 * t + 1, term, alibi_t))
        pad = jnp.zeros((LANES - n_blocks - SUBLANES, seq), F32)
        aux_t = jnp.concatenate([sel_t, alibi_t, pad], axis=0).astype(BF16)
        qx_scr[sub, :, :LANES] = q_head
        qx_scr[sub, :, LANES:] = _transpose_to_rows(aux_t).astype(qx_scr.dtype)

    _flash_sweep((qi_tab, t_tab),
                 lambda sub, start: qx_scr[sub, pl.ds(start, tile), :],
                 lambda sub, start: kx_scr[pl.ds(start, tile), :],
                 lambda sub, start: v_ref[0, pl.ds(start, tile), :],
                 scr, n_sub=2, tile=tile, n_q=n_q)
    _store_head_pair(o_ref, scr, tile=tile, n_q=n_q)


def _attention(kind, q_arr, k_arr, v_arr, extras, *, batch, seq, q_col, k_col, v_col):
    tile = ATTN_TILE
    n_q = seq // tile
    mla = kind == "mla"
    n_sub = 1 if mla else 2
    groups = MLA_HEADS if mla else FOX_HEADS // 2
    dk = QK_DEPTH if mla else LANES
    in_specs = [
        pl.BlockSpec((1, seq, dk), lambda b, g, *_: (b, 0, q_col + g)),
        pl.BlockSpec((1, seq, dk), lambda b, g, *_: (b, 0, k_col + g)),
        pl.BlockSpec((1, seq, LANES), lambda b, g, *_: (b, 0, v_col + g)),
    ]
    sweep_scr = _SweepScratch.shapes(n_sub, tile, n_q)
    operand_scr = [pltpu.VMEM((seq, QK_DEPTH), BF16), pltpu.VMEM((n_sub, seq, QK_DEPTH), BF16)]
    if kind == "fox":
        body = functools.partial(_fox_kernel, tile=tile, n_q=n_q)
        in_specs.append(pl.BlockSpec((1, seq, LANES), lambda b, g, *_: (b, 0, 0)))
        scratch = operand_scr + sweep_scr
    elif mla:
        body = functools.partial(_mla_kernel, tile=tile, n_q=n_q)
        scratch = sweep_scr
    else:
        body = functools.partial(_moba_kernel, tile=tile, n_q=n_q, n_blocks=seq // MOBA_BLOCK)
        in_specs.append(pl.BlockSpec((seq, LANES), lambda b, g, *_: (0, 0)))
        in_specs.append(pl.BlockSpec((N_TERMS * MOBA_HEADS, LANES), lambda b, g, *_: (0, 0)))
        scratch = operand_scr + sweep_scr
    return pl.pallas_call(
        body,
        grid_spec=pltpu.PrefetchScalarGridSpec(
            num_scalar_prefetch=2,
            grid=(batch, groups),
            in_specs=in_specs,
            out_specs=pl.BlockSpec((1, seq, LANES), lambda b, g, *_: (b, 0, g)),
            scratch_shapes=scratch),
        out_shape=jax.ShapeDtypeStruct((batch, seq, groups * LANES), BF16),
        compiler_params=pltpu.CompilerParams(
            dimension_semantics=("arbitrary", "arbitrary"),
            vmem_limit_bytes=VMEM_LIMIT_BYTES),
        name=kind + "_attn",
    )(*_sweep_items(n_q), q_arr, k_arr, v_arr, *extras)


def _combine_kernel(ya_ref, yb_ref, yc_ref, za_ref, zb_ref, zc_ref, x_ref, og_ref, w_ref, fg_ref,
                    o_ref, *, final):
    acc = x_ref[...]
    groups = ((ya_ref, za_ref), (yb_ref, zb_ref), (yc_ref, zc_ref))
    for idx, (y_ref, z_ref) in enumerate(groups):
        lo, hi = idx * FOX_WIDTH, (idx + 1) * FOX_WIDTH
        gated = _rms(y_ref[...].astype(F32), og_ref[:, lo:hi]) * z_ref[...].astype(F32)
        acc = acc + jnp.dot(gated.astype(BF16), w_ref[lo:hi, :], preferred_element_type=F32)
    if final:
        acc = _rms(acc, fg_ref[...])
    o_ref[...] = acc


def _combine(ya, yb, yc, proj, x2d, out_g, w_out, final_g, *, final):
    tokens = x2d.shape[0]
    tm = COMBINE_ROWS
    width = FOX_WIDTH
    zcol = lambda c: c * LANES // width
    y_spec = pl.BlockSpec((tm, width), lambda i: (i, 0))
    return pl.pallas_call(
        functools.partial(_combine_kernel, final=final),
        grid=(tokens // tm,),
        in_specs=[
            y_spec, y_spec, y_spec,
            pl.BlockSpec((tm, width), lambda i: (i, zcol(COL_FZ))),
            pl.BlockSpec((tm, width), lambda i: (i, zcol(COL_MZ))),
            pl.BlockSpec((tm, width), lambda i: (i, zcol(COL_BZ))),
            pl.BlockSpec((tm, D_MODEL), lambda i: (i, 0)),
            pl.BlockSpec((1, D_MIX), lambda i: (0, 0)),
            pl.BlockSpec((D_MIX, D_MODEL), lambda i: (0, 0)),
            pl.BlockSpec((1, D_MODEL), lambda i: (0, 0)),
        ],
        out_specs=pl.BlockSpec((tm, D_MODEL), lambda i: (i, 0)),
        out_shape=jax.ShapeDtypeStruct((tokens, D_MODEL), F32),
        compiler_params=pltpu.CompilerParams(
            dimension_semantics=("arbitrary",),
            vmem_limit_bytes=VMEM_LIMIT_BYTES),
        name="combine",
    )(ya, yb, yc, proj, proj, proj, x2d, out_g, w_out, final_g)


def _rot_cols(w):
    half = MLA_ROPE // 2
    return jnp.concatenate([-w[..., half:], w[..., :half]], axis=-1)


def _pack_layer(w_in, w_uq, w_ukv):
    cuts = np.cumsum(IN_SIZES)[:-1].tolist()
    fq, fk, fv, ff, fz, cq, ckv, kr, mz, bq, bk, bv, bz = jnp.split(w_in, cuts, axis=1)
    w_packed = jnp.concatenate(
        [fq, fk, fv, fz, bq, bk, bv, bz, mz, cq, ckv, kr, _rot_cols(kr)], axis=1).astype(BF16)
    wff_t = jnp.zeros((BF16_SUBLANES, D_MODEL), F32).at[:FOX_HEADS].set(ff.T).astype(BF16)
    wq = w_uq.reshape(MLA_Q_RANK, MLA_HEADS, MLA_QK)
    nope, rope = wq[..., :MLA_NOPE], wq[..., MLA_NOPE:]
    wq_packed = jnp.concatenate([nope, rope, _rot_cols(rope)], axis=-1)
    wq_packed = wq_packed.reshape(MLA_Q_RANK, MLA_HEADS * QK_DEPTH).astype(BF16)
    wkv = w_ukv.reshape(MLA_KV_RANK, MLA_HEADS, MLA_NOPE + MLA_V)
    wkv_packed = jnp.concatenate(
        [wkv[..., :MLA_NOPE].reshape(MLA_KV_RANK, MLA_WIDTH),
         wkv[..., MLA_NOPE:].reshape(MLA_KV_RANK, MLA_WIDTH)], axis=1).astype(BF16)
    return w_packed, wff_t, wq_packed, wkv_packed


def _moba_key_aux(seq):
    pos = np.arange(seq)
    aux = np.zeros((seq, LANES), np.float32)
    n_blocks = seq // MOBA_BLOCK
    aux[pos, pos // MOBA_BLOCK] = 1.0
    for t in range(N_TERMS):
        aux[:, n_blocks + 2 * t] = pos // MOBA_BLOCK
        aux[:, n_blocks + 2 * t + 1] = pos % MOBA_BLOCK
    return jnp.asarray(aux, BF16)


def kernel(x, ln_g, w_in, fox_b_f, mla_q_g, mla_w_uq, mla_kv_g, mla_w_ukv, out_g, w_out, final_g):
    batch, seq, _ = x.shape
    depth = ln_g.shape[0]
    tokens = batch * seq

    inv = ROPE_THETA ** (-jnp.arange(0, MLA_ROPE, 2, dtype=F32) / MLA_ROPE)
    ang = jnp.arange(seq, dtype=F32)[:, None] * inv[None, :]
    ang = jnp.concatenate([ang, ang], axis=-1)
    cs = jnp.concatenate([jnp.cos(ang), jnp.sin(ang)], axis=-1)
    slopes = 2.0 ** (-8.0 * jnp.arange(1, MOBA_HEADS + 1, dtype=F32) / MOBA_HEADS)
    slope_terms = jnp.concatenate([t.astype(F32) for t in _bf16_terms(slopes * LOG2E)])
    slope_terms = jnp.broadcast_to(slope_terms[:, None], (N_TERMS * MOBA_HEADS, LANES))
    moba_kaux = _moba_key_aux(seq)

    x2d = x.reshape(tokens, D_MODEL)
    for l in range(depth):
        w_packed, wff_t, wq_packed, wkv_packed = _pack_layer(w_in[l], mla_w_uq[l], mla_w_ukv[l])
        bf_col = jnp.zeros((BF16_SUBLANES, 1), F32).at[:FOX_HEADS, 0].set(fox_b_f[l])
        proj, caux = _inproj(x2d, ln_g[l][None, :], w_packed, wff_t, bf_col, seq=seq)
        q_cat, k_cat, v_mla = _mla_up(proj, cs, mla_q_g[l][None, :], mla_kv_g[l][None, :],
                                      wq_packed, wkv_packed, seq=seq)
        proj3 = proj.reshape(batch, seq, PACKED_COLS)
        ya = _attention("fox", proj3, proj3, proj3, [caux.reshape(batch, seq, LANES)],
                        batch=batch, seq=seq, q_col=COL_FQ, k_col=COL_FK, v_col=COL_FV)
        yb = _attention("mla", q_cat.reshape(batch, seq, -1), k_cat.reshape(batch, seq, -1),
                        v_mla.reshape(batch, seq, -1), [], batch=batch, seq=seq,
                        q_col=0, k_col=0, v_col=0)
        yc = _attention("moba", proj3, proj3, proj3, [moba_kaux, slope_terms],
                        batch=batch, seq=seq, q_col=COL_BQ, k_col=COL_BK, v_col=COL_BV)
        x2d = _combine(ya.reshape(tokens, -1), yb.reshape(tokens, -1), yc.reshape(tokens, -1),
                       proj, x2d, out_g[l][None, :], w_out[l].astype(BF16), final_g[None, :],
                       final=(l == depth - 1))
    return x2d.reshape(batch, seq, D_MODEL)
```
